```python
import math
import jax, jax.numpy as jnp
from jax import lax
import numpy as np

D_MODEL = 1024
BATCH = 4
SEQ = 8192
DEPTH = 4

N_META = 16
C_CONV = D_MODEL
CONV_K = 31
HEAD_DIM = 64
N_HEADS = D_MODEL // (2 * HEAD_DIM)
V_DIM = 2 * HEAD_DIM
Q_W = N_HEADS * 2 * HEAD_DIM
K_W = N_HEADS * 2 * HEAD_DIM
V_W = N_HEADS * V_DIM
GATE_W = 2 * D_MODEL
N_IN = 2 * C_CONV + Q_W + K_W + V_W + GATE_W
SPLITS = (2 * C_CONV, 2 * C_CONV + Q_W, 2 * C_CONV + Q_W + K_W, 2 * C_CONV + Q_W + K_W + V_W)
D_FF = 4 * D_MODEL
ROPE_THETA = 10000.0
BLOCK_Q = 128
NORM_EPS = 1e-6
SUBLN_EPS = 1e-5

kernel_name = 'hybrid_conformer_conv_diff_attn_gated'


def rms_norm(t, g, eps=NORM_EPS):
    t32 = t.astype(jnp.float32)
    y = t32 * lax.rsqrt(jnp.mean(t32 * t32, axis=-1, keepdims=True) + eps)
    return (y * g.astype(jnp.float32)).astype(t.dtype)


def layer_norm(t, g, b, eps=NORM_EPS):
    t32 = t.astype(jnp.float32)
    mu = jnp.mean(t32, axis=-1, keepdims=True)
    var = jnp.mean(jnp.square(t32 - mu), axis=-1, keepdims=True)
    y = (t32 - mu) * lax.rsqrt(var + eps)
    return (y * g.astype(jnp.float32) + b.astype(jnp.float32)).astype(t.dtype)


def rope_tables(n_pos, dim):
    inv = 1.0 / (ROPE_THETA ** (jnp.arange(0, dim, 2, dtype=jnp.float32) / dim))
    ang = jnp.arange(n_pos, dtype=jnp.float32)[:, None] * inv[None, :]
    return jnp.cos(ang), jnp.sin(ang)


def apply_rope(t, cos, sin):
    c = cos[None, :, None, None, :]
    s = sin[None, :, None, None, :]
    t32 = t.astype(jnp.float32)
    t1, t2 = jnp.split(t32, 2, axis=-1)
    return jnp.concatenate([t1 * c - t2 * s, t2 * c + t1 * s], axis=-1).astype(t.dtype)


def causal_depthwise_conv(a, w, b):
    y = lax.conv_general_dilated(a, w.reshape(CONV_K, 1, C_CONV).astype(a.dtype), window_strides=(1,),
                                 padding=[(CONV_K - 1, 0)], dimension_numbers=('NWC', 'WIO', 'NWC'),
                                 feature_group_count=C_CONV)
    return y + b


def diff_attention(q, k, v, lam):
    L = q.shape[1]
    scale = HEAD_DIM ** -0.5
    outs = []
    for i in range(L // BLOCK_Q):
        kend = (i + 1) * BLOCK_Q
        qs = q[:, i * BLOCK_Q:kend]
        ks = k[:, :kend]
        vs = v[:, :kend]
        s = jnp.einsum('bqhcd,bkhcd->bchqk', qs, ks).astype(jnp.float32) * scale
        q_pos = i * BLOCK_Q + jnp.arange(BLOCK_Q)
        k_pos = jnp.arange(kend)
        mask = k_pos[None, :] <= q_pos[:, None]
        s = jnp.where(mask, s, -jnp.inf)
        p = jax.nn.softmax(s, axis=-1)
        a = p[:, 0] - lam * p[:, 1]
        outs.append(jnp.einsum('bhqk,bkhe->bqhe', a.astype(vs.dtype), vs))
    return jnp.concatenate(outs, axis=1)


def setup_inputs(seed: int = 0) -> dict:
    key = jax.random.key(seed)
    ks = jax.random.split(key, 20)
    f32 = jnp.float32
    nrm = lambda k, shape, s: jax.random.normal(k, shape, f32) * s
    return {
        'x': nrm(ks[0], (BATCH, SEQ, D_MODEL), 1.0),
        'meta_tokens': nrm(ks[1], (N_META, D_MODEL), 1.0),
        'norm_mix': 1.0 + nrm(ks[2], (DEPTH, D_MODEL), 0.02),
        'w_in': nrm(ks[3], (DEPTH, D_MODEL, N_IN), D_MODEL ** -0.5),
        'b_glu': nrm(ks[4], (DEPTH, 2 * C_CONV), 0.02),
        'conv_w': nrm(ks[5], (DEPTH, CONV_K, C_CONV), CONV_K ** -0.5),
        'conv_b': nrm(ks[6], (DEPTH, C_CONV), 0.02),
        'conv_ln_g': 1.0 + nrm(ks[7], (DEPTH, C_CONV), 0.02),
        'conv_ln_b': nrm(ks[8], (DEPTH, C_CONV), 0.02),
        'w_pw': nrm(ks[9], (DEPTH, C_CONV, D_MODEL), C_CONV ** -0.5),
        'b_pw': nrm(ks[10], (DEPTH, D_MODEL), 0.02),
        'q_norm': 1.0 + nrm(ks[11], (DEPTH, HEAD_DIM), 0.02),
        'k_norm': 1.0 + nrm(ks[12], (DEPTH, HEAD_DIM), 0.02),
        'lambdas': nrm(ks[13], (DEPTH, 4, HEAD_DIM), 0.1),
        'subln': 1.0 + nrm(ks[14], (DEPTH, V_DIM), 0.02),
        'w_o': nrm(ks[15], (DEPTH, D_MODEL, D_MODEL), D_MODEL ** -0.5),
        'norm_mlp': 1.0 + nrm(ks[16], (DEPTH, D_MODEL), 0.02),
        'w_up': nrm(ks[17], (DEPTH, D_MODEL, D_FF), D_MODEL ** -0.5),
        'w_down': nrm(ks[18], (DEPTH, D_FF, D_MODEL), 0.5 * D_FF ** -0.5),
    }


def reference(x, meta_tokens, norm_mix, w_in, b_glu, conv_w, conv_b, conv_ln_g, conv_ln_b, w_pw, b_pw,
              q_norm, k_norm, lambdas, subln, w_o, norm_mlp, w_up, w_down):
    B, S, D = x.shape
    L = S + N_META
    Lp = -(-L // BLOCK_Q) * BLOCK_Q
    meta = jnp.broadcast_to(meta_tokens[None].astype(x.dtype), (B, N_META, D))
    h = jnp.concatenate([meta, x, jnp.zeros((B, Lp - L, D), x.dtype)], axis=1)
    cos, sin = rope_tables(Lp, HEAD_DIM)

    for l in range(DEPTH):
        lam_init = 0.8 - 0.6 * math.exp(-0.3 * l)
        u = rms_norm(h, norm_mix[l])
        proj = u @ w_in[l]
        glu_in, q, k, v, gates = jnp.split(proj, SPLITS, axis=-1)

        a = glu_in + b_glu[l]
        a = a[..., :C_CONV] * jax.nn.sigmoid(a[..., C_CONV:])
        a = causal_depthwise_conv(a, conv_w[l], conv_b[l])
        a = jax.nn.silu(layer_norm(a, conv_ln_g[l], conv_ln_b[l]))
        y_conv = a @ w_pw[l] + b_pw[l]

        q = apply_rope(rms_norm(q.reshape(B, Lp, N_HEADS, 2, HEAD_DIM), q_norm[l]), cos, sin)
        k = apply_rope(rms_norm(k.reshape(B, Lp, N_HEADS, 2, HEAD_DIM), k_norm[l]), cos, sin)
        v = v.reshape(B, Lp, N_HEADS, V_DIM)
        lp = lambdas[l].astype(jnp.float32)
        lam = jnp.exp(jnp.sum(lp[0] * lp[1])) - jnp.exp(jnp.sum(lp[2] * lp[3])) + lam_init
        o = diff_attention(q, k, v, lam)
        o = rms_norm(o, subln[l], SUBLN_EPS) * (1.0 - lam_init)
        y_attn = o.reshape(B, Lp, D)

        g_conv, g_attn = jnp.split(gates, 2, axis=-1)
        merged = jax.nn.sigmoid(g_conv) * y_conv + jax.nn.sigmoid(g_attn) * y_attn
        h = h + merged @ w_o[l]

        u = rms_norm(h, norm_mlp[l])
        h = h + jnp.square(jax.nn.relu(u @ w_up[l])) @ w_down[l]

    return h[:, N_META:N_META + S]
```

```python
import functools
import math

import jax
import jax.numpy as jnp
from jax import lax
from jax.experimental import pallas as pl
from jax.experimental.pallas import tpu as pltpu

F32 = jnp.float32
BF16 = jnp.bfloat16

N_META = 16
HEAD_DIM = 64
V_DIM = 2 * HEAD_DIM
CONV_K = 31
ROPE_THETA = 10000.0
NORM_EPS = 1e-6
SUBLN_EPS = 1e-5
LAM_INIT_BASE, LAM_INIT_AMP, LAM_INIT_RATE = 0.8, 0.6, 0.3

MXU_TILE = 256
SUBLANES = 8
FRONT = MXU_TILE
PAD0 = FRONT - N_META
SEQ_TILE = 3 * MXU_TILE
CONV_TILE = MXU_TILE
CONV_HALO = 32
CONV_ROWS = 64
CONV_LANES = 256
VMEM_LIMIT = 56 * 1024 * 1024
NEG_BIG = -1e30


def _sigmoid(x):
    return 1.0 / (1.0 + jnp.exp(-x))


def _dot(a, b):
    return jnp.dot(a, b, preferred_element_type=F32)


def _rmsnorm_rows(x, g, eps):
    ms = jnp.mean(x * x, axis=-1, keepdims=True)
    return x * lax.rsqrt(ms + eps) * g


def _inproj_kernel(h_ref, gmix_ref, w_ref, bglu_ref, qg_ref, kg_ref, cos_ref, sin_ref,
                   a_ref, q1_ref, q2_ref, k_ref, vt_ref, gc_ref, ga_ref, *, qscale):
    d = h_ref.shape[-1]
    tm = h_ref.shape[-2]
    nsub = tm // MXU_TILE
    u = _rmsnorm_rows(h_ref[0], gmix_ref[...], NORM_EPS).astype(BF16)
    cos_t = cos_ref[...]
    sin_t = sin_ref[...]

    def proj(off):
        return _dot(u, w_ref[:, off:off + MXU_TILE])

    for c in range(0, d, MXU_TILE):
        lin = proj(c) + bglu_ref[:, c:c + MXU_TILE]
        gate = proj(d + c) + bglu_ref[:, d + c:d + c + MXU_TILE]
        a_ref[0, :, c:c + MXU_TILE] = (lin * _sigmoid(gate)).astype(BF16)

    half = HEAD_DIM // 2

    def norm_rope(g_t, gain_col, scale):
        r = lax.rsqrt(jnp.mean(g_t * g_t, axis=0, keepdims=True) + NORM_EPS)
        if scale != 1.0:
            r = r * scale
        gn = g_t * r * gain_col
        x1, x2 = gn[:half], gn[half:]
        return x1 * cos_t - x2 * sin_t, x2 * cos_t + x1 * sin_t

    zeros_half = jnp.zeros((HEAD_DIM, tm), BF16)
    for c in range(0, d, MXU_TILE):
        q_t = proj(2 * d + c).T
        k_t = proj(3 * d + c).T
        for hh in range(MXU_TILE // V_DIM):
            r0 = hh * V_DIM
            col = c + r0
            a1, a2 = norm_rope(q_t[r0:r0 + HEAD_DIM], qg_ref[...], qscale)
            b1, b2 = norm_rope(q_t[r0 + HEAD_DIM:r0 + V_DIM], qg_ref[...], qscale)
            q1z = jnp.concatenate([a1.astype(BF16), a2.astype(BF16), zeros_half], axis=0)
            q2z = jnp.concatenate([zeros_half, b1.astype(BF16), b2.astype(BF16)], axis=0)
            for s in range(nsub):
                q1_ref[0, s, col:col + V_DIM, :] = q1z[:, s * MXU_TILE:(s + 1) * MXU_TILE]
                q2_ref[0, s, col:col + V_DIM, :] = q2z[:, s * MXU_TILE:(s + 1) * MXU_TILE]
            a1, a2 = norm_rope(k_t[r0:r0 + HEAD_DIM], kg_ref[...], 1.0)
            b1, b2 = norm_rope(k_t[r0 + HEAD_DIM:r0 + V_DIM], kg_ref[...], 1.0)
            kh_t = jnp.concatenate([a1, a2, b1, b2], axis=0)
            k_ref[0, :, col:col + V_DIM] = kh_t.T.astype(BF16)

    for c in range(0, d, MXU_TILE):
        v_t = proj(4 * d + c).T.astype(BF16)
        for s in range(nsub):
            vt_ref[0, s, c:c + MXU_TILE, :] = v_t[:, s * MXU_TILE:(s + 1) * MXU_TILE]

    for c in range(0, d, MXU_TILE):
        gc_ref[0, :, c:c + MXU_TILE] = _sigmoid(proj(5 * d + c)).astype(BF16)
        ga_ref[0, :, c:c + MXU_TILE] = _sigmoid(proj(6 * d + c)).astype(BF16)


def _in_proj(h, gmix, w_in, b_glu, qg, kg, cos_t, sin_t, layer):
    b, lq, d = h.shape
    tm = SEQ_TILE
    nt = lq // tm
    nblk = lq // MXU_TILE
    n_in = w_in.shape[-1]
    qscale = (HEAD_DIM ** -0.5) * math.log2(math.e)
    tok = pl.BlockSpec((1, tm, d), lambda bi, i: (bi, i, 0))
    tsp = pl.BlockSpec((1, tm // MXU_TILE, d, MXU_TILE), lambda bi, i: (bi, i, 0, 0))
    lay = lambda *shape: pl.BlockSpec((None,) + shape, lambda bi, i: (layer,) + (0,) * len(shape))
    tok_shape = jax.ShapeDtypeStruct((b, lq, d), BF16)
    tsp_shape = jax.ShapeDtypeStruct((b, nblk, d, MXU_TILE), BF16)
    return pl.pallas_call(
        functools.partial(_inproj_kernel, qscale=qscale),
        grid=(b, nt),
        in_specs=[
            tok,
            lay(1, d),
            pl.BlockSpec((None, d, n_in), lambda bi, i: (layer, 0, 0), pipeline_mode=pl.Buffered(1)),
            lay(1, 2 * d),
            lay(HEAD_DIM, 1),
            lay(HEAD_DIM, 1),
            pl.BlockSpec((HEAD_DIM // 2, tm), lambda bi, i: (0, i)),
            pl.BlockSpec((HEAD_DIM // 2, tm), lambda bi, i: (0, i)),
        ],
        out_specs=[tok, tsp, tsp, tok, tsp, tok, tok],
        out_shape=[tok_shape, tsp_shape, tsp_shape, tok_shape, tsp_shape, tok_shape, tok_shape],
        compiler_params=pltpu.CompilerParams(
            dimension_semantics=("parallel", "parallel"), vmem_limit_bytes=VMEM_LIMIT),
        name="in_proj",
    )(h, gmix, w_in, b_glu, qg, kg, cos_t, sin_t)


def _conv_kernel(cur_ref, halo_ref, cw_ref, cb_ref, lng_ref, lnb_ref, wpw_ref, bpw_ref, gc_ref,
                 out_ref, z_ref, y_ref):
    i = pl.program_id(1)
    tt = cur_ref.shape[-2]
    d = cur_ref.shape[-1]
    hrow = lax.broadcasted_iota(jnp.int32, (CONV_HALO, 1), 0) + (i * tt - CONV_HALO)
    halo = jnp.where(hrow >= PAD0, halo_ref[0].astype(F32), 0.0)
    row = lax.broadcasted_iota(jnp.int32, (tt, 1), 0) + i * tt
    cur = jnp.where(row >= PAD0, cur_ref[0].astype(F32), 0.0)
    z_ref[0, 0:CONV_HALO, :] = halo
    z_ref[0, CONV_HALO:CONV_HALO + tt, :] = cur
    nshift = tt + CONV_HALO - SUBLANES
    for sh in range(1, SUBLANES):
        z_ref[sh, 0:nshift, :] = z_ref[0, sh:sh + nshift, :]
    base = CONV_HALO - (CONV_K - 1)
    for lc in range(0, d, CONV_LANES):
        for rc in range(0, tt, CONV_ROWS):
            acc = jnp.zeros((CONV_ROWS, CONV_LANES), F32)
            for j in range(CONV_K):
                a8, b8 = divmod(base + j, SUBLANES)
                r0 = rc + a8 * SUBLANES
                acc = acc + z_ref[b8, r0:r0 + CONV_ROWS, lc:lc + CONV_LANES] * cw_ref[j:j + 1, lc:lc + CONV_LANES]
            y_ref[rc:rc + CONV_ROWS, lc:lc + CONV_LANES] = acc
    y = y_ref[...] + cb_ref[...]
    mu = jnp.mean(y, axis=-1, keepdims=True)
    yc = y - mu
    var = jnp.mean(yc * yc, axis=-1, keepdims=True)
    yn = yc * lax.rsqrt(var + NORM_EPS) * lng_ref[...] + lnb_ref[...]
    act = (yn * _sigmoid(yn)).astype(BF16)
    yo = _dot(act, wpw_ref[...]) + bpw_ref[...]
    out_ref[0] = (yo * gc_ref[0].astype(F32)).astype(BF16)


def _conv_branch(a, conv_w, conv_b, ln_g, ln_b, w_pw, b_pw, gc, layer):
    b, lq, d = a.shape
    tt = CONV_TILE
    hb = tt // CONV_HALO
    tok = pl.BlockSpec((1, tt, d), lambda bi, i: (bi, i, 0))
    lay = lambda *shape: pl.BlockSpec((None,) + shape, lambda bi, i: (layer,) + (0,) * len(shape))
    return pl.pallas_call(
        _conv_kernel,
        grid=(b, lq // tt),
        in_specs=[
            tok,
            pl.BlockSpec((1, CONV_HALO, d), lambda bi, i: (bi, jnp.maximum(i * hb - 1, 0), 0)),
            lay(CONV_K, d), lay(1, d), lay(1, d), lay(1, d), lay(d, d), lay(1, d),
            tok,
        ],
        out_specs=tok,
        out_shape=jax.ShapeDtypeStruct((b, lq, d), BF16),
        scratch_shapes=[pltpu.VMEM((SUBLANES, tt + CONV_HALO, d), F32), pltpu.VMEM((tt, d), F32)],
        compiler_params=pltpu.CompilerParams(
            dimension_semantics=("parallel", "parallel"), vmem_limit_bytes=VMEM_LIMIT),
        name="conv_branch",
    )(a, a, conv_w, conv_b, ln_g, ln_b, w_pw, b_pw, gc)


def _attn_kernel(q1_ref, q2_ref, k_ref, vt_ref, ga_ref, lam_ref, sub_ref, out_ref,
                 acc_ref, m_ref, l_ref, *, lam_init):
    i = pl.program_id(2)
    nsub = SEQ_TILE // MXU_TILE
    q_refs = (q1_ref, q2_ref)

    m_ref[...] = jnp.full(m_ref.shape, NEG_BIG, F32)
    l_ref[...] = jnp.zeros(l_ref.shape, F32)
    acc_ref[...] = jnp.zeros(acc_ref.shape, F32)

    def update(c, n, s, vt_tiles):
        m_old = m_ref[c, n]
        m_new = jnp.maximum(m_old, jnp.max(s, axis=0, keepdims=True))
        alpha = jnp.exp2(m_old - m_new)
        p = jnp.exp2(s - m_new)
        l_ref[c, n] = alpha * l_ref[c, n] + jnp.sum(p, axis=0, keepdims=True)
        pb = p.astype(BF16)
        pv = _dot(vt_tiles[0], pb[0:MXU_TILE])
        for t in range(1, len(vt_tiles)):
            pv = pv + _dot(vt_tiles[t], pb[t * MXU_TILE:(t + 1) * MXU_TILE])
        acc_ref[c, n] = alpha * acc_ref[c, n] + pv
        m_ref[c, n] = m_new

    def full_block(j, key_ok):
        kb = k_ref[0, pl.ds(pl.multiple_of(j * SEQ_TILE, SEQ_TILE), SEQ_TILE), :]
        vt_tiles = [vt_ref[0, j * nsub + t] for t in range(nsub)]
        for n in range(nsub):
            for c in range(2):
                s = _dot(kb, q_refs[c][0, n])
                if key_ok is not None:
                    s = jnp.where(key_ok, s, NEG_BIG)
                update(c, n, s, vt_tiles)

    @pl.when(i > 0)
    def _():
        krow = lax.broadcasted_iota(jnp.int32, (SEQ_TILE, MXU_TILE), 0)
        full_block(0, krow >= PAD0)

    def body(j, carry):
        full_block(j, None)
        return carry

    lax.fori_loop(1, i, body, 0)

    base = pl.multiple_of(i * SEQ_TILE, SEQ_TILE)
    for n in range(nsub):
        nk = (n + 1) * MXU_TILE
        kb = k_ref[0, pl.ds(base, nk), :]
        vt_tiles = [vt_ref[0, i * nsub + t] for t in range(n + 1)]
        kpos = lax.broadcasted_iota(jnp.int32, (nk, MXU_TILE), 0) + i * SEQ_TILE
        qpos = lax.broadcasted_iota(jnp.int32, (nk, MXU_TILE), 1) + (i * SEQ_TILE + n * MXU_TILE)
        ok = (kpos <= qpos) & ((kpos >= PAD0) | (qpos < PAD0))
        for c in range(2):
            s = jnp.where(ok, _dot(kb, q_refs[c][0, n]), NEG_BIG)
            update(c, n, s, vt_tiles)

    lp = lam_ref[...]
    lam = (jnp.exp(jnp.sum(lp[0:1] * lp[1:2], axis=1, keepdims=True))
           - jnp.exp(jnp.sum(lp[2:3] * lp[3:4], axis=1, keepdims=True)) + lam_init)
    for n in range(nsub):
        o = acc_ref[0, n] * (1.0 / l_ref[0, n]) - lam * (acc_ref[1, n] * (1.0 / l_ref[1, n]))
        r = lax.rsqrt(jnp.mean(o * o, axis=0, keepdims=True) + SUBLN_EPS)
        on = o * r * sub_ref[...] * (1.0 - lam_init)
        gate = ga_ref[0, n * MXU_TILE:(n + 1) * MXU_TILE, :].astype(F32)
        out_ref[0, n * MXU_TILE:(n + 1) * MXU_TILE, :] = (on.T * gate).astype(BF16)


def _attention(q1, q2, k, vt, ga, lambdas, subln, layer, lam_init):
    b, lq, d = k.shape
    nh = d // V_DIM
    nq = lq // SEQ_TILE
    nblk = lq // MXU_TILE
    nsub = SEQ_TILE // MXU_TILE
    qspec = pl.BlockSpec((1, nsub, V_DIM, MXU_TILE), lambda bi, h, i: (bi, i, h, 0))
    tokh = pl.BlockSpec((1, SEQ_TILE, V_DIM), lambda bi, h, i: (bi, i, h))
    return pl.pallas_call(
        functools.partial(_attn_kernel, lam_init=lam_init),
        grid=(b, nh, nq),
        in_specs=[
            qspec, qspec,
            pl.BlockSpec((1, lq, V_DIM), lambda bi, h, i: (bi, 0, h)),
            pl.BlockSpec((1, nblk, V_DIM, MXU_TILE), lambda bi, h, i: (bi, 0, h, 0)),
            tokh,
            pl.BlockSpec((None, 4, HEAD_DIM), lambda bi, h, i: (layer, 0, 0)),
            pl.BlockSpec((None, V_DIM, 1), lambda bi, h, i: (layer, 0, 0)),
        ],
        out_specs=tokh,
        out_shape=jax.ShapeDtypeStruct((b, lq, d), BF16),
        scratch_shapes=[
            pltpu.VMEM((2, nsub, V_DIM, MXU_TILE), F32),
            pltpu.VMEM((2, nsub, 1, MXU_TILE), F32),
            pltpu.VMEM((2, nsub, 1, MXU_TILE), F32),
        ],
        compiler_params=pltpu.CompilerParams(
            dimension_semantics=("parallel", "parallel", "arbitrary"), vmem_limit_bytes=VMEM_LIMIT),
        name="diff_attn",
    )(q1, q2, k, vt, ga, lambdas, subln)


def _mlp_kernel(h_ref, yc_ref, ya_ref, wo_ref, gm_ref, wup_ref, wdn_ref, out_ref):
    d = h_ref.shape[-1]
    dff = wup_ref.shape[-1]
    merged = (yc_ref[0].astype(F32) + ya_ref[0].astype(F32)).astype(BF16)
    h1 = h_ref[0] + _dot(merged, wo_ref[...])
    u = _rmsnorm_rows(h1, gm_ref[...], NORM_EPS).astype(BF16)
    acc = h1
    for c in range(0, dff, d):
        t = jnp.maximum(_dot(u, wup_ref[:, c:c + d]), 0.0)
        acc = acc + _dot((t * t).astype(BF16), wdn_ref[c:c + d, :])
    out_ref[0] = acc


def _out_mlp(h, yc, ya, w_o, gm, w_up, w_down, layer):
    b, lq, d = h.shape
    dff = w_up.shape[-1]
    tm = SEQ_TILE
    tok = pl.BlockSpec((1, tm, d), lambda bi, i: (bi, i, 0))
    wspec = lambda *shape: pl.BlockSpec((None,) + shape, lambda bi, i: (layer,) + (0,) * len(shape),
                                        pipeline_mode=pl.Buffered(1))
    return pl.pallas_call(
        _mlp_kernel,
        grid=(b, lq // tm),
        in_specs=[tok, tok, tok, wspec(d, d),
                  pl.BlockSpec((None, 1, d), lambda bi, i: (layer, 0, 0)),
                  wspec(d, dff), wspec(dff, d)],
        out_specs=tok,
        out_shape=jax.ShapeDtypeStruct((b, lq, d), F32),
        compiler_params=pltpu.CompilerParams(
            dimension_semantics=("parallel", "parallel"), vmem_limit_bytes=VMEM_LIMIT),
        name="out_mlp",
    )(h, yc, ya, w_o, gm, w_up, w_down)


def _rope_tables_t(lq):
    inv = 1.0 / (ROPE_THETA ** (jnp.arange(0, HEAD_DIM, 2, dtype=F32) / HEAD_DIM))
    pos = (jnp.arange(lq, dtype=jnp.int32) - PAD0).astype(F32)
    ang = pos[:, None] * inv[None, :]
    return jnp.cos(ang).T, jnp.sin(ang).T


def kernel(x, meta_tokens, norm_mix, w_in, b_glu, conv_w, conv_b, conv_ln_g, conv_ln_b, w_pw, b_pw, q_norm, k_norm, lambdas, subln, w_o, norm_mlp, w_up, w_down):
    b, s, d = x.shape
    depth = w_in.shape[0]
    lq = FRONT + s
    assert d % MXU_TILE == 0 and lq % SEQ_TILE == 0, (s, d)

    meta = jnp.broadcast_to(meta_tokens[None].astype(x.dtype), (b, N_META, d))
    h = jnp.concatenate([jnp.zeros((b, PAD0, d), x.dtype), meta, x], axis=1)
    cos_t, sin_t = _rope_tables_t(lq)

    w_in_b = w_in.astype(BF16)
    w_pw_b = w_pw.astype(BF16)
    w_o_b = w_o.astype(BF16)
    w_up_b = w_up.astype(BF16)
    w_down_b = w_down.astype(BF16)
    row = lambda p: p[:, None, :]
    col = lambda p: p[:, :, None]

    for l in range(depth):
        lam_init = LAM_INIT_BASE - LAM_INIT_AMP * math.exp(-LAM_INIT_RATE * l)
        a, q1, q2, k, vt, gc, ga = _in_proj(h, row(norm_mix), w_in_b, row(b_glu), col(q_norm), col(k_norm),
                                            cos_t, sin_t, l)
        yc = _conv_branch(a, conv_w, row(conv_b), row(conv_ln_g), row(conv_ln_b), w_pw_b, row(b_pw), gc, l)
        ya = _attention(q1, q2, k, vt, ga, lambdas, col(subln), l, lam_init)
        h = _out_mlp(h, yc, ya, w_o_b, row(norm_mlp), w_up_b, w_down_b, l)

    return h[:, FRONT:, :]
```

```python
import functools
import math

import jax
import jax.numpy as jnp
from jax import lax
from jax.experimental import pallas as pl
from jax.experimental.pallas import tpu as pltpu

F32 = jnp.float32
BF16 = jnp.bfloat16

N_META = 16
HEAD_DIM = 64
V_DIM = 2 * HEAD_DIM
CONV_K = 31
ROPE_THETA = 10000.0
NORM_EPS = 1e-6
SUBLN_EPS = 1e-5
LAM_INIT_BASE, LAM_INIT_AMP, LAM_INIT_RATE = 0.8, 0.6, 0.3

MXU_TILE = 256
SUBLANES = 8
FRONT = MXU_TILE
PAD0 = FRONT - N_META
SEQ_TILE = 3 * MXU_TILE
CONV_TILE = MXU_TILE
CONV_HALO = 32
CONV_ROWS = 64
CONV_LANES = 256
VMEM_LIMIT = 56 * 1024 * 1024
NEG_BIG = -1e30


def _sigmoid(x):
    return 1.0 / (1.0 + jnp.exp(-x))


def _dot(a, b):
    return jnp.dot(a, b, preferred_element_type=F32)


def _rmsnorm_rows(x, g, eps):
    ms = jnp.mean(x * x, axis=-1, keepdims=True)
    return x * lax.rsqrt(ms + eps) * g


def _inproj_kernel(h_ref, gmix_ref, w_ref, bglu_ref, qg_ref, kg_ref, cos_ref, sin_ref,
                   a_ref, q1_ref, q2_ref, k_ref, vt_ref, gc_ref, ga_ref, *, qscale):
    d = h_ref.shape[-1]
    tm = h_ref.shape[-2]
    nsub = tm // MXU_TILE
    u = _rmsnorm_rows(h_ref[0], gmix_ref[...], NORM_EPS).astype(BF16)
    cos_t = cos_ref[...]
    sin_t = sin_ref[...]

    def proj(off):
        return _dot(u, w_ref[:, off:off + MXU_TILE])

    for c in range(0, d, MXU_TILE):
        lin = proj(c) + bglu_ref[:, c:c + MXU_TILE]
        gate = proj(d + c) + bglu_ref[:, d + c:d + c + MXU_TILE]
        a_ref[0, :, c:c + MXU_TILE] = (lin * _sigmoid(gate)).astype(BF16)

    half = HEAD_DIM // 2

    def norm_rope(g_t, gain_col, scale):
        r = lax.rsqrt(jnp.mean(g_t * g_t, axis=0, keepdims=True) + NORM_EPS)
        if scale != 1.0:
            r = r * scale
        gn = g_t * r * gain_col
        x1, x2 = gn[:half], gn[half:]
        return x1 * cos_t - x2 * sin_t, x2 * cos_t + x1 * sin_t

    zeros_half = jnp.zeros((HEAD_DIM, tm), BF16)
    for c in range(0, d, MXU_TILE):
        q_t = proj(2 * d + c).T
        k_t = proj(3 * d + c).T
        for hh in range(MXU_TILE // V_DIM):
            r0 = hh * V_DIM
            col = c + r0
            a1, a2 = norm_rope(q_t[r0:r0 + HEAD_DIM], qg_ref[...], qscale)
            b1, b2 = norm_rope(q_t[r0 + HEAD_DIM:r0 + V_DIM], qg_ref[...], qscale)
            q1z = jnp.concatenate([a1.astype(BF16), a2.astype(BF16), zeros_half], axis=0)
            q2z = jnp.concatenate([zeros_half, b1.astype(BF16), b2.astype(BF16)], axis=0)
            for s in range(nsub):
                q1_ref[0, s, col:col + V_DIM, :] = q1z[:, s * MXU_TILE:(s + 1) * MXU_TILE]
                q2_ref[0, s, col:col + V_DIM, :] = q2z[:, s * MXU_TILE:(s + 1) * MXU_TILE]
            a1, a2 = norm_rope(k_t[r0:r0 + HEAD_DIM], kg_ref[...], 1.0)
            b1, b2 = norm_rope(k_t[r0 + HEAD_DIM:r0 + V_DIM], kg_ref[...], 1.0)
            kh_t = jnp.concatenate([a1, a2, b1, b2], axis=0)
            k_ref[0, :, col:col + V_DIM] = kh_t.T.astype(BF16)

    for c in range(0, d, MXU_TILE):
        v_t = proj(4 * d + c).T.astype(BF16)
        for s in range(nsub):
            vt_ref[0, s, c:c + MXU_TILE, :] = v_t[:, s * MXU_TILE:(s + 1) * MXU_TILE]

    for c in range(0, d, MXU_TILE):
        gc_ref[0, :, c:c + MXU_TILE] = _sigmoid(proj(5 * d + c)).astype(BF16)
        ga_ref[0, :, c:c + MXU_TILE] = _sigmoid(proj(6 * d + c)).astype(BF16)


def _in_proj(h, gmix, w_in, b_glu, qg, kg, cos_t, sin_t, layer):
    b, lq, d = h.shape
    tm = SEQ_TILE
    nt = lq // tm
    nblk = lq // MXU_TILE
    n_in = w_in.shape[-1]
    qscale = (HEAD_DIM ** -0.5) * math.log2(math.e)
    tok = pl.BlockSpec((1, tm, d), lambda bi, i: (bi, i, 0))
    tsp = pl.BlockSpec((1, tm // MXU_TILE, d, MXU_TILE), lambda bi, i: (bi, i, 0, 0))
    lay = lambda *shape: pl.BlockSpec((None,) + shape, lambda bi, i: (layer,) + (0,) * len(shape))
    tok_shape = jax.ShapeDtypeStruct((b, lq, d), BF16)
    tsp_shape = jax.ShapeDtypeStruct((b, nblk, d, MXU_TILE), BF16)
    return pl.pallas_call(
        functools.partial(_inproj_kernel, qscale=qscale),
        grid=(b, nt),
        in_specs=[
            tok,
            lay(1, d),
            pl.BlockSpec((None, d, n_in), lambda bi, i: (layer, 0, 0), pipeline_mode=pl.Buffered(1)),
            lay(1, 2 * d),
            lay(HEAD_DIM, 1),
            lay(HEAD_DIM, 1),
            pl.BlockSpec((HEAD_DIM // 2, tm), lambda bi, i: (0, i)),
            pl.BlockSpec((HEAD_DIM // 2, tm), lambda bi, i: (0, i)),
        ],
        out_specs=[tok, tsp, tsp, tok, tsp, tok, tok],
        out_shape=[tok_shape, tsp_shape, tsp_shape, tok_shape, tsp_shape, tok_shape, tok_shape],
        compiler_params=pltpu.CompilerParams(
            dimension_semantics=("parallel", "parallel"), vmem_limit_bytes=VMEM_LIMIT),
        name="in_proj",
    )(h, gmix, w_in, b_glu, qg, kg, cos_t, sin_t)


def _conv_kernel(cur_ref, halo_ref, cw_ref, cb_ref, lng_ref, lnb_ref, wpw_ref, bpw_ref, gc_ref,
                 out_ref, z_ref, y_ref):
    i = pl.program_id(1)
    tt = cur_ref.shape[-2]
    d = cur_ref.shape[-1]
    hrow = lax.broadcasted_iota(jnp.int32, (CONV_HALO, 1), 0) + (i * tt - CONV_HALO)
    halo = jnp.where(hrow >= PAD0, halo_ref[0].astype(F32), 0.0)
    row = lax.broadcasted_iota(jnp.int32, (tt, 1), 0) + i * tt
    cur = jnp.where(row >= PAD0, cur_ref[0].astype(F32), 0.0)
    z_ref[0, 0:CONV_HALO, :] = halo
    z_ref[0, CONV_HALO:CONV_HALO + tt, :] = cur
    nshift = tt + CONV_HALO - SUBLANES
    for sh in range(1, SUBLANES):
        z_ref[sh, 0:nshift, :] = z_ref[0, sh:sh + nshift, :]
    base = CONV_HALO - (CONV_K - 1)
    for lc in range(0, d, CONV_LANES):
        for rc in range(0, tt, CONV_ROWS):
            acc = jnp.zeros((CONV_ROWS, CONV_LANES), F32)
            for j in range(CONV_K):
                a8, b8 = divmod(base + j, SUBLANES)
                r0 = rc + a8 * SUBLANES
                acc = acc + z_ref[b8, r0:r0 + CONV_ROWS, lc:lc + CONV_LANES] * cw_ref[j:j + 1, lc:lc + CONV_LANES]
            y_ref[rc:rc + CONV_ROWS, lc:lc + CONV_LANES] = acc
    y = y_ref[...] + cb_ref[...]
    mu = jnp.mean(y, axis=-1, keepdims=True)
    yc = y - mu
    var = jnp.mean(yc * yc, axis=-1, keepdims=True)
    yn = yc * lax.rsqrt(var + NORM_EPS) * lng_ref[...] + lnb_ref[...]
    act = (yn * _sigmoid(yn)).astype(BF16)
    yo = _dot(act, wpw_ref[...]) + bpw_ref[...]
    out_ref[0] = (yo * gc_ref[0].astype(F32)).astype(BF16)


def _conv_branch(a, conv_w, conv_b, ln_g, ln_b, w_pw, b_pw, gc, layer):
    b, lq, d = a.shape
    tt = CONV_TILE
    hb = tt // CONV_HALO
    tok = pl.BlockSpec((1, tt, d), lambda bi, i: (bi, i, 0))
    lay = lambda *shape: pl.BlockSpec((None,) + shape, lambda bi, i: (layer,) + (0,) * len(shape))
    return pl.pallas_call(
        _conv_kernel,
        grid=(b, lq // tt),
        in_specs=[
            tok,
            pl.BlockSpec((1, CONV_HALO, d), lambda bi, i: (bi, jnp.maximum(i * hb - 1, 0), 0)),
            lay(CONV_K, d), lay(1, d), lay(1, d), lay(1, d), lay(d, d), lay(1, d),
            tok,
        ],
        out_specs=tok,
        out_shape=jax.ShapeDtypeStruct((b, lq, d), BF16),
        scratch_shapes=[pltpu.VMEM((SUBLANES, tt + CONV_HALO, d), F32), pltpu.VMEM((tt, d), F32)],
        compiler_params=pltpu.CompilerParams(
            dimension_semantics=("parallel", "parallel"), vmem_limit_bytes=VMEM_LIMIT),
        name="conv_branch",
    )(a, a, conv_w, conv_b, ln_g, ln_b, w_pw, b_pw, gc)


def _attn_kernel(q1_ref, q2_ref, k_ref, vt_ref, ga_ref, lam_ref, sub_ref, out_ref,
                 acc_ref, m_ref, l_ref, *, lam_init):
    i = pl.program_id(2)
    nsub = SEQ_TILE // MXU_TILE
    q_refs = (q1_ref, q2_ref)

    m_ref[...] = jnp.full(m_ref.shape, NEG_BIG, F32)
    l_ref[...] = jnp.zeros(l_ref.shape, F32)
    acc_ref[...] = jnp.zeros(acc_ref.shape, F32)

    def softmax_stage(c, n, s):
        m_old = m_ref[c, n]
        m_new = jnp.maximum(m_old, jnp.max(s, axis=0, keepdims=True))
        alpha = jnp.exp2(m_old - m_new)
        p = jnp.exp2(s - m_new)
        l_ref[c, n] = alpha * l_ref[c, n] + jnp.sum(p, axis=0, keepdims=True)
        m_ref[c, n] = m_new
        return alpha, p.astype(BF16)

    def pv_stage(c, n, alpha, pb, vt_tiles):
        pv = _dot(vt_tiles[0], pb[0:MXU_TILE])
        for t in range(1, len(vt_tiles)):
            pv = pv + _dot(vt_tiles[t], pb[t * MXU_TILE:(t + 1) * MXU_TILE])
        acc_ref[c, n] = alpha * acc_ref[c, n] + pv

    def run_pipelined(combos):
        scores, probs = {}, {}
        for t in range(len(combos) + 2):
            if t < len(combos):
                scores[t] = combos[t][2]()
            if 0 <= t - 2 < len(combos):
                c, n, _, vt_tiles = combos[t - 2]
                pv_stage(c, n, *probs.pop(t - 2), vt_tiles)
            if 0 <= t - 1 < len(combos):
                c, n, _, _ = combos[t - 1]
                probs[t - 1] = softmax_stage(c, n, scores.pop(t - 1))

    def full_block(j, key_ok):
        kb = k_ref[0, pl.ds(pl.multiple_of(j * SEQ_TILE, SEQ_TILE), SEQ_TILE), :]
        vt_tiles = [vt_ref[0, j * nsub + t] for t in range(nsub)]

        def score_fn(c, n):
            s = _dot(kb, q_refs[c][0, n])
            return s if key_ok is None else jnp.where(key_ok, s, NEG_BIG)

        run_pipelined([(c, n, functools.partial(score_fn, c, n), vt_tiles)
                       for n in range(nsub) for c in range(2)])

    @pl.when(i > 0)
    def _():
        krow = lax.broadcasted_iota(jnp.int32, (SEQ_TILE, MXU_TILE), 0)
        full_block(0, krow >= PAD0)

    def body(j, carry):
        full_block(j, None)
        return carry

    lax.fori_loop(1, i, body, 0)

    base = pl.multiple_of(i * SEQ_TILE, SEQ_TILE)
    combos = []
    for n in range(nsub):
        nk = (n + 1) * MXU_TILE
        kb = k_ref[0, pl.ds(base, nk), :]
        vt_tiles = [vt_ref[0, i * nsub + t] for t in range(n + 1)]
        kpos = lax.broadcasted_iota(jnp.int32, (nk, MXU_TILE), 0) + i * SEQ_TILE
        qpos = lax.broadcasted_iota(jnp.int32, (nk, MXU_TILE), 1) + (i * SEQ_TILE + n * MXU_TILE)
        ok = (kpos <= qpos) & ((kpos >= PAD0) | (qpos < PAD0))

        def score_fn(c, n=n, kb=kb, ok=ok):
            return jnp.where(ok, _dot(kb, q_refs[c][0, n]), NEG_BIG)

        combos += [(c, n, functools.partial(score_fn, c), vt_tiles) for c in range(2)]
    run_pipelined(combos)

    lp = lam_ref[...]
    lam = (jnp.exp(jnp.sum(lp[0:1] * lp[1:2], axis=1, keepdims=True))
           - jnp.exp(jnp.sum(lp[2:3] * lp[3:4], axis=1, keepdims=True)) + lam_init)
    for n in range(nsub):
        o = acc_ref[0, n] * (1.0 / l_ref[0, n]) - lam * (acc_ref[1, n] * (1.0 / l_ref[1, n]))
        r = lax.rsqrt(jnp.mean(o * o, axis=0, keepdims=True) + SUBLN_EPS)
        on = o * r * sub_ref[...] * (1.0 - lam_init)
        gate = ga_ref[0, n * MXU_TILE:(n + 1) * MXU_TILE, :].astype(F32)
        out_ref[0, n * MXU_TILE:(n + 1) * MXU_TILE, :] = (on.T * gate).astype(BF16)


def _attention(q1, q2, k, vt, ga, lambdas, subln, layer, lam_init):
    b, lq, d = k.shape
    nh = d // V_DIM
    nq = lq // SEQ_TILE
    nblk = lq // MXU_TILE
    nsub = SEQ_TILE // MXU_TILE
    qspec = pl.BlockSpec((1, nsub, V_DIM, MXU_TILE), lambda bi, h, i: (bi, i, h, 0))
    tokh = pl.BlockSpec((1, SEQ_TILE, V_DIM), lambda bi, h, i: (bi, i, h))
    return pl.pallas_call(
        functools.partial(_attn_kernel, lam_init=lam_init),
        grid=(b, nh, nq),
        in_specs=[
            qspec, qspec,
            pl.BlockSpec((1, lq, V_DIM), lambda bi, h, i: (bi, 0, h)),
            pl.BlockSpec((1, nblk, V_DIM, MXU_TILE), lambda bi, h, i: (bi, 0, h, 0)),
            tokh,
            pl.BlockSpec((None, 4, HEAD_DIM), lambda bi, h, i: (layer, 0, 0)),
            pl.BlockSpec((None, V_DIM, 1), lambda bi, h, i: (layer, 0, 0)),
        ],
        out_specs=tokh,
        out_shape=jax.ShapeDtypeStruct((b, lq, d), BF16),
        scratch_shapes=[
            pltpu.VMEM((2, nsub, V_DIM, MXU_TILE), F32),
            pltpu.VMEM((2, nsub, 1, MXU_TILE), F32),
            pltpu.VMEM((2, nsub, 1, MXU_TILE), F32),
        ],
        compiler_params=pltpu.CompilerParams(
            dimension_semantics=("parallel", "parallel", "arbitrary"), vmem_limit_bytes=VMEM_LIMIT),
        name="diff_attn",
    )(q1, q2, k, vt, ga, lambdas, subln)


def _mlp_kernel(h_ref, yc_ref, ya_ref, wo_ref, gm_ref, wup_ref, wdn_ref, out_ref):
    d = h_ref.shape[-1]
    dff = wup_ref.shape[-1]
    merged = (yc_ref[0].astype(F32) + ya_ref[0].astype(F32)).astype(BF16)
    h1 = h_ref[0] + _dot(merged, wo_ref[...])
    u = _rmsnorm_rows(h1, gm_ref[...], NORM_EPS).astype(BF16)
    acc = h1
    for c in range(0, dff, d):
        t = jnp.maximum(_dot(u, wup_ref[:, c:c + d]), 0.0)
        acc = acc + _dot((t * t).astype(BF16), wdn_ref[c:c + d, :])
    out_ref[0] = acc


def _out_mlp(h, yc, ya, w_o, gm, w_up, w_down, layer):
    b, lq, d = h.shape
    dff = w_up.shape[-1]
    tm = SEQ_TILE
    tok = pl.BlockSpec((1, tm, d), lambda bi, i: (bi, i, 0))
    wspec = lambda *shape: pl.BlockSpec((None,) + shape, lambda bi, i: (layer,) + (0,) * len(shape),
                                        pipeline_mode=pl.Buffered(1))
    return pl.pallas_call(
        _mlp_kernel,
        grid=(b, lq // tm),
        in_specs=[tok, tok, tok, wspec(d, d),
                  pl.BlockSpec((None, 1, d), lambda bi, i: (layer, 0, 0)),
                  wspec(d, dff), wspec(dff, d)],
        out_specs=tok,
        out_shape=jax.ShapeDtypeStruct((b, lq, d), F32),
        compiler_params=pltpu.CompilerParams(
            dimension_semantics=("parallel", "parallel"), vmem_limit_bytes=VMEM_LIMIT),
        name="out_mlp",
    )(h, yc, ya, w_o, gm, w_up, w_down)


def _rope_tables_t(lq):
    inv = 1.0 / (ROPE_THETA ** (jnp.arange(0, HEAD_DIM, 2, dtype=F32) / HEAD_DIM))
    pos = (jnp.arange(lq, dtype=jnp.int32) - PAD0).astype(F32)
    ang = pos[:, None] * inv[None, :]
    return jnp.cos(ang).T, jnp.sin(ang).T


def kernel(x, meta_tokens, norm_mix, w_in, b_glu, conv_w, conv_b, conv_ln_g, conv_ln_b, w_pw, b_pw, q_norm, k_norm, lambdas, subln, w_o, norm_mlp, w_up, w_down):
    b, s, d = x.shape
    depth = w_in.shape[0]
    lq = FRONT + s
    assert d % MXU_TILE == 0 and lq % SEQ_TILE == 0, (s, d)

    meta = jnp.broadcast_to(meta_tokens[None].astype(x.dtype), (b, N_META, d))
    h = jnp.concatenate([jnp.zeros((b, PAD0, d), x.dtype), meta, x], axis=1)
    cos_t, sin_t = _rope_tables_t(lq)

    w_in_b = w_in.astype(BF16)
    w_pw_b = w_pw.astype(BF16)
    w_o_b = w_o.astype(BF16)
    w_up_b = w_up.astype(BF16)
    w_down_b = w_down.astype(BF16)
    row = lambda p: p[:, None, :]
    col = lambda p: p[:, :, None]

    for l in range(depth):
        lam_init = LAM_INIT_BASE - LAM_INIT_AMP * math.exp(-LAM_INIT_RATE * l)
        a, q1, q2, k, vt, gc, ga = _in_proj(h, row(norm_mix), w_in_b, row(b_glu), col(q_norm), col(k_norm),
                                            cos_t, sin_t, l)
        yc = _conv_branch(a, conv_w, row(conv_b), row(conv_ln_g), row(conv_ln_b), w_pw_b, row(b_pw), gc, l)
        ya = _attention(q1, q2, k, vt, ga, lambdas, col(subln), l, lam_init)
        h = _out_mlp(h, yc, ya, w_o_b, row(norm_mlp), w_up_b, w_down_b, l)

    return h[:, FRONT:, :]
```

```python
import functools
import math

import jax
import jax.numpy as jnp
from jax import lax
from jax.experimental import pallas as pl
from jax.experimental.pallas import tpu as pltpu

F32 = jnp.float32
BF16 = jnp.bfloat16

N_META = 16
HEAD_DIM = 64
V_DIM = 2 * HEAD_DIM
SUM_ROWS = 16
VT_ROWS = V_DIM + SUM_ROWS
CONV_K = 31
ROPE_THETA = 10000.0
NORM_EPS = 1e-6
SUBLN_EPS = 1e-5
LAM_INIT_BASE, LAM_INIT_AMP, LAM_INIT_RATE = 0.8, 0.6, 0.3

MXU_TILE = 256
SUBLANES = 8
FRONT = MXU_TILE
PAD0 = FRONT - N_META
SEQ_TILE = 3 * MXU_TILE
CONV_TILE = MXU_TILE
CONV_HALO = 32
CONV_ROWS = 64
CONV_LANES = 256
VMEM_LIMIT = 56 * 1024 * 1024
NEG_BIG = -1e30
M_INIT = -1e29


def _sigmoid(x):
    return 1.0 / (1.0 + jnp.exp(-x))


def _dot(a, b):
    return jnp.dot(a, b, preferred_element_type=F32)


def _rmsnorm_rows(x, g, eps):
    ms = jnp.mean(x * x, axis=-1, keepdims=True)
    return x * lax.rsqrt(ms + eps) * g


def _inproj_kernel(h_ref, gmix_ref, w_ref, bglu_ref, qg_ref, kg_ref, cos_ref, sin_ref,
                   a_ref, q1_ref, q2_ref, k_ref, vt_ref, gc_ref, ga_ref, *, qscale):
    d = h_ref.shape[-1]
    tm = h_ref.shape[-2]
    nsub = tm // MXU_TILE
    u = _rmsnorm_rows(h_ref[0], gmix_ref[...], NORM_EPS).astype(BF16)
    cos_t = cos_ref[...]
    sin_t = sin_ref[...]

    def proj(off):
        return _dot(u, w_ref[:, off:off + MXU_TILE])

    for c in range(0, d, MXU_TILE):
        lin = proj(c) + bglu_ref[:, c:c + MXU_TILE]
        gate = proj(d + c) + bglu_ref[:, d + c:d + c + MXU_TILE]
        a_ref[0, :, c:c + MXU_TILE] = (lin * _sigmoid(gate)).astype(BF16)

    half = HEAD_DIM // 2

    def norm_rope(g_t, gain_col, scale):
        r = lax.rsqrt(jnp.mean(g_t * g_t, axis=0, keepdims=True) + NORM_EPS)
        if scale != 1.0:
            r = r * scale
        gn = g_t * r * gain_col
        x1, x2 = gn[:half], gn[half:]
        return x1 * cos_t - x2 * sin_t, x2 * cos_t + x1 * sin_t

    zeros_half = jnp.zeros((HEAD_DIM, tm), BF16)
    for c in range(0, d, MXU_TILE):
        q_t = proj(2 * d + c).T
        k_t = proj(3 * d + c).T
        for hh in range(MXU_TILE // V_DIM):
            r0 = hh * V_DIM
            col = c + r0
            a1, a2 = norm_rope(q_t[r0:r0 + HEAD_DIM], qg_ref[...], qscale)
            b1, b2 = norm_rope(q_t[r0 + HEAD_DIM:r0 + V_DIM], qg_ref[...], qscale)
            q1z = jnp.concatenate([a1.astype(BF16), a2.astype(BF16), zeros_half], axis=0)
            q2z = jnp.concatenate([zeros_half, b1.astype(BF16), b2.astype(BF16)], axis=0)
            for s in range(nsub):
                q1_ref[0, s, col:col + V_DIM, :] = q1z[:, s * MXU_TILE:(s + 1) * MXU_TILE]
                q2_ref[0, s, col:col + V_DIM, :] = q2z[:, s * MXU_TILE:(s + 1) * MXU_TILE]
            a1, a2 = norm_rope(k_t[r0:r0 + HEAD_DIM], kg_ref[...], 1.0)
            b1, b2 = norm_rope(k_t[r0 + HEAD_DIM:r0 + V_DIM], kg_ref[...], 1.0)
            kh_t = jnp.concatenate([a1, a2, b1, b2], axis=0)
            k_ref[0, :, col:col + V_DIM] = kh_t.T.astype(BF16)

    ones = jnp.ones((SUM_ROWS, MXU_TILE), BF16)
    for c in range(0, d, MXU_TILE):
        v_t = proj(4 * d + c).T.astype(BF16)
        for hh in range(MXU_TILE // V_DIM):
            r0 = (c // V_DIM + hh) * VT_ROWS
            for s in range(nsub):
                vt_ref[0, s, r0:r0 + V_DIM, :] = v_t[hh * V_DIM:(hh + 1) * V_DIM, s * MXU_TILE:(s + 1) * MXU_TILE]
                vt_ref[0, s, r0 + V_DIM:r0 + VT_ROWS, :] = ones

    for c in range(0, d, MXU_TILE):
        gc_ref[0, :, c:c + MXU_TILE] = _sigmoid(proj(5 * d + c)).astype(BF16)
        ga_ref[0, :, c:c + MXU_TILE] = _sigmoid(proj(6 * d + c)).astype(BF16)


def _in_proj(h, gmix, w_in, b_glu, qg, kg, cos_t, sin_t, layer):
    b, lq, d = h.shape
    tm = SEQ_TILE
    nt = lq // tm
    nblk = lq // MXU_TILE
    n_in = w_in.shape[-1]
    qscale = (HEAD_DIM ** -0.5) * math.log2(math.e)
    tok = pl.BlockSpec((1, tm, d), lambda bi, i: (bi, i, 0))
    tsp = pl.BlockSpec((1, tm // MXU_TILE, d, MXU_TILE), lambda bi, i: (bi, i, 0, 0))
    lay = lambda *shape: pl.BlockSpec((None,) + shape, lambda bi, i: (layer,) + (0,) * len(shape))
    tok_shape = jax.ShapeDtypeStruct((b, lq, d), BF16)
    tsp_shape = jax.ShapeDtypeStruct((b, nblk, d, MXU_TILE), BF16)
    vt_rows = (d // V_DIM) * VT_ROWS
    vsp = pl.BlockSpec((1, tm // MXU_TILE, vt_rows, MXU_TILE), lambda bi, i: (bi, i, 0, 0))
    vsp_shape = jax.ShapeDtypeStruct((b, nblk, vt_rows, MXU_TILE), BF16)
    return pl.pallas_call(
        functools.partial(_inproj_kernel, qscale=qscale),
        grid=(b, nt),
        in_specs=[
            tok,
            lay(1, d),
            pl.BlockSpec((None, d, n_in), lambda bi, i: (layer, 0, 0), pipeline_mode=pl.Buffered(1)),
            lay(1, 2 * d),
            lay(HEAD_DIM, 1),
            lay(HEAD_DIM, 1),
            pl.BlockSpec((HEAD_DIM // 2, tm), lambda bi, i: (0, i)),
            pl.BlockSpec((HEAD_DIM // 2, tm), lambda bi, i: (0, i)),
        ],
        out_specs=[tok, tsp, tsp, tok, vsp, tok, tok],
        out_shape=[tok_shape, tsp_shape, tsp_shape, tok_shape, vsp_shape, tok_shape, tok_shape],
        compiler_params=pltpu.CompilerParams(
            dimension_semantics=("parallel", "parallel"), vmem_limit_bytes=VMEM_LIMIT),
        name="in_proj",
    )(h, gmix, w_in, b_glu, qg, kg, cos_t, sin_t)


def _conv_kernel(cur_ref, halo_ref, cw_ref, cb_ref, lng_ref, lnb_ref, wpw_ref, bpw_ref, gc_ref,
                 out_ref, z_ref, y_ref):
    i = pl.program_id(1)
    tt = cur_ref.shape[-2]
    d = cur_ref.shape[-1]
    hrow = lax.broadcasted_iota(jnp.int32, (CONV_HALO, 1), 0) + (i * tt - CONV_HALO)
    halo = jnp.where(hrow >= PAD0, halo_ref[0].astype(F32), 0.0)
    row = lax.broadcasted_iota(jnp.int32, (tt, 1), 0) + i * tt
    cur = jnp.where(row >= PAD0, cur_ref[0].astype(F32), 0.0)
    z_ref[0, 0:CONV_HALO, :] = halo
    z_ref[0, CONV_HALO:CONV_HALO + tt, :] = cur
    nshift = tt + CONV_HALO - SUBLANES
    for sh in range(1, SUBLANES):
        z_ref[sh, 0:nshift, :] = z_ref[0, sh:sh + nshift, :]
    base = CONV_HALO - (CONV_K - 1)
    for lc in range(0, d, CONV_LANES):
        for rc in range(0, tt, CONV_ROWS):
            acc = jnp.zeros((CONV_ROWS, CONV_LANES), F32)
            for j in range(CONV_K):
                a8, b8 = divmod(base + j, SUBLANES)
                r0 = rc + a8 * SUBLANES
                w_rows = jnp.tile(cw_ref[j, :, lc:lc + CONV_LANES], (CONV_ROWS // SUBLANES, 1))
                acc = acc + z_ref[b8, r0:r0 + CONV_ROWS, lc:lc + CONV_LANES] * w_rows
            y_ref[rc:rc + CONV_ROWS, lc:lc + CONV_LANES] = acc
    y = y_ref[...] + cb_ref[...]
    mu = jnp.mean(y, axis=-1, keepdims=True)
    yc = y - mu
    var = jnp.mean(yc * yc, axis=-1, keepdims=True)
    yn = yc * lax.rsqrt(var + NORM_EPS) * lng_ref[...] + lnb_ref[...]
    act = (yn * _sigmoid(yn)).astype(BF16)
    yo = _dot(act, wpw_ref[...]) + bpw_ref[...]
    out_ref[0] = (yo * gc_ref[0].astype(F32)).astype(BF16)


def _conv_branch(a, conv_w, conv_b, ln_g, ln_b, w_pw, b_pw, gc, layer):
    b, lq, d = a.shape
    tt = CONV_TILE
    hb = tt // CONV_HALO
    tok = pl.BlockSpec((1, tt, d), lambda bi, i: (bi, i, 0))
    lay = lambda *shape: pl.BlockSpec((None,) + shape, lambda bi, i: (layer,) + (0,) * len(shape))
    return pl.pallas_call(
        _conv_kernel,
        grid=(b, lq // tt),
        in_specs=[
            tok,
            pl.BlockSpec((1, CONV_HALO, d), lambda bi, i: (bi, jnp.maximum(i * hb - 1, 0), 0)),
            lay(CONV_K, SUBLANES, d), lay(1, d), lay(1, d), lay(1, d), lay(d, d), lay(1, d),
            tok,
        ],
        out_specs=tok,
        out_shape=jax.ShapeDtypeStruct((b, lq, d), BF16),
        scratch_shapes=[pltpu.VMEM((SUBLANES, tt + CONV_HALO, d), F32), pltpu.VMEM((tt, d), F32)],
        compiler_params=pltpu.CompilerParams(
            dimension_semantics=("parallel", "parallel"), vmem_limit_bytes=VMEM_LIMIT),
        name="conv_branch",
    )(a, a, conv_w, conv_b, ln_g, ln_b, w_pw, b_pw, gc)


def _attn_kernel(q1_ref, q2_ref, k_ref, vt_ref, ga_ref, lam_ref, sub_ref, out_ref,
                 acc_ref, m_ref, s_pend_ref, p_pend_ref, a_pend_ref, *, lam_init):
    i = pl.program_id(2)
    nsub = SEQ_TILE // MXU_TILE
    q_refs = (q1_ref, q2_ref)

    m_ref[...] = jnp.full(m_ref.shape, M_INIT, F32)
    acc_ref[...] = jnp.zeros(acc_ref.shape, F32)
    s_pend_ref[...] = jnp.full(s_pend_ref.shape, NEG_BIG, F32)
    p_pend_ref[...] = jnp.zeros(p_pend_ref.shape, BF16)
    a_pend_ref[...] = jnp.ones(a_pend_ref.shape, F32)

    def softmax_stage(c, n, s):
        m_old = m_ref[c, n]
        m_new = jnp.maximum(m_old, jnp.max(s, axis=0, keepdims=True))
        alpha = jnp.exp2(m_old - m_new)
        p = jnp.exp2(s - m_new)
        m_ref[c, n] = m_new
        return alpha, p.astype(BF16)

    def pv_stage(c, n, alpha, pb, vt_tiles):
        pv = _dot(vt_tiles[0], pb[0:MXU_TILE])
        for t in range(1, len(vt_tiles)):
            pv = pv + _dot(vt_tiles[t], pb[t * MXU_TILE:(t + 1) * MXU_TILE])
        acc_ref[c, n] = alpha * acc_ref[c, n] + pv

    def vt_block(j):
        return [vt_ref[0, j * nsub + t] for t in range(nsub)]

    pend_pv, pend_sm = (0, nsub - 1), (1, nsub - 1)

    def run_block(combos, vt_prev, last):
        ncomb = len(combos)
        scores, probs = {}, {}
        for t in range(ncomb):
            scores[t] = combos[t][2]()
            if t == 0:
                pv_stage(*pend_pv, a_pend_ref[...], p_pend_ref[...], vt_prev)
                probs[-1] = softmax_stage(*pend_sm, s_pend_ref[...])
            else:
                if t == 1:
                    pv_stage(*pend_sm, *probs.pop(-1), vt_prev)
                else:
                    c, n, _, vt_tiles = combos[t - 2]
                    pv_stage(c, n, *probs.pop(t - 2), vt_tiles)
                c, n, _, _ = combos[t - 1]
                probs[t - 1] = softmax_stage(c, n, scores.pop(t - 1))
        if last:
            c, n, _, vt_tiles = combos[ncomb - 2]
            pv_stage(c, n, *probs.pop(ncomb - 2), vt_tiles)
            c, n, _, vt_tiles = combos[ncomb - 1]
            pv_stage(c, n, *softmax_stage(c, n, scores.pop(ncomb - 1)), vt_tiles)
        else:
            assert (combos[ncomb - 2][:2], combos[ncomb - 1][:2]) == (pend_pv, pend_sm)
            a_pend_ref[...], p_pend_ref[...] = probs.pop(ncomb - 2)
            s_pend_ref[...] = scores.pop(ncomb - 1)

    def block_combos(j, has_pad):
        kb = k_ref[0, pl.ds(pl.multiple_of(j * SEQ_TILE, SEQ_TILE), SEQ_TILE), :]

        def score_fn(c, n):
            s = _dot(kb, q_refs[c][0, n])
            if has_pad:
                key_ok = lax.broadcasted_iota(jnp.int32, (MXU_TILE, MXU_TILE), 0) >= PAD0
                s = jnp.concatenate([jnp.where(key_ok, s[:MXU_TILE], NEG_BIG), s[MXU_TILE:]], axis=0)
            return s

        return [(c, n, functools.partial(score_fn, c, n), vt_block(j)) for n in range(nsub) for c in range(2)]

    @pl.when(i > 0)
    def _():
        run_block(block_combos(0, True), vt_block(0), last=False)

    def pair_body(p, carry):
        j = 1 + 2 * p
        run_block(block_combos(j, False) + block_combos(j + 1, False), vt_block(j - 1), last=False)
        return carry

    lax.fori_loop(0, (i - 1) // 2, pair_body, 0)

    @pl.when((i > 1) & ((i - 1) % 2 == 1))
    def _():
        run_block(block_combos(i - 1, False), vt_block(i - 2), last=False)

    base = pl.multiple_of(i * SEQ_TILE, SEQ_TILE)
    krow = lax.broadcasted_iota(jnp.int32, (MXU_TILE, MXU_TILE), 0)
    qcol = lax.broadcasted_iota(jnp.int32, (MXU_TILE, MXU_TILE), 1)
    causal = krow <= qcol
    pad_end = jnp.where(i == 0, PAD0, 0)
    combos = []
    for n in reversed(range(nsub)):
        nk = (n + 1) * MXU_TILE
        kb = k_ref[0, pl.ds(base, nk), :]
        first_ok = (krow >= pad_end) | (qcol < pad_end) if n == 0 else krow >= pad_end

        def score_fn(c, n=n, kb=kb, first_ok=first_ok):
            s = _dot(kb, q_refs[c][0, n])
            tiles = [s[t * MXU_TILE:(t + 1) * MXU_TILE] for t in range(n + 1)]
            tiles[n] = jnp.where(causal, tiles[n], NEG_BIG)
            tiles[0] = jnp.where(first_ok, tiles[0], NEG_BIG)
            return jnp.concatenate(tiles, axis=0)

        combos += [(c, n, functools.partial(score_fn, c), vt_block(i)[:n + 1]) for c in range(2)]
    run_block(combos, vt_block(jnp.maximum(i - 1, 0)), last=True)

    lp = lam_ref[...]
    lam = (jnp.exp(jnp.sum(lp[0:1] * lp[1:2], axis=1, keepdims=True))
           - jnp.exp(jnp.sum(lp[2:3] * lp[3:4], axis=1, keepdims=True)) + lam_init)
    for n in range(nsub):
        inv_l1 = 1.0 / acc_ref[0, n, V_DIM:V_DIM + 1, :]
        inv_l2 = 1.0 / acc_ref[1, n, V_DIM:V_DIM + 1, :]
        o = acc_ref[0, n, 0:V_DIM, :] * inv_l1 - lam * (acc_ref[1, n, 0:V_DIM, :] * inv_l2)
        r = lax.rsqrt(jnp.mean(o * o, axis=0, keepdims=True) + SUBLN_EPS)
        on = o * r * sub_ref[...] * (1.0 - lam_init)
        gate = ga_ref[0, n * MXU_TILE:(n + 1) * MXU_TILE, :].astype(F32)
        out_ref[0, n * MXU_TILE:(n + 1) * MXU_TILE, :] = (on.T * gate).astype(BF16)


def _attention(q1, q2, k, vt, ga, lambdas, subln, layer, lam_init):
    b, lq, d = k.shape
    nh = d // V_DIM
    nq = lq // SEQ_TILE
    nblk = lq // MXU_TILE
    nsub = SEQ_TILE // MXU_TILE
    qspec = pl.BlockSpec((1, nsub, V_DIM, MXU_TILE), lambda bi, h, i: (bi, i, h, 0))
    tokh = pl.BlockSpec((1, SEQ_TILE, V_DIM), lambda bi, h, i: (bi, i, h))
    return pl.pallas_call(
        functools.partial(_attn_kernel, lam_init=lam_init),
        grid=(b, nh, nq),
        in_specs=[
            qspec, qspec,
            pl.BlockSpec((1, lq, V_DIM), lambda bi, h, i: (bi, 0, h)),
            pl.BlockSpec((1, nblk, VT_ROWS, MXU_TILE), lambda bi, h, i: (bi, 0, h, 0)),
            tokh,
            pl.BlockSpec((None, 4, HEAD_DIM), lambda bi, h, i: (layer, 0, 0)),
            pl.BlockSpec((None, V_DIM, 1), lambda bi, h, i: (layer, 0, 0)),
        ],
        out_specs=tokh,
        out_shape=jax.ShapeDtypeStruct((b, lq, d), BF16),
        scratch_shapes=[
            pltpu.VMEM((2, nsub, VT_ROWS, MXU_TILE), F32),
            pltpu.VMEM((2, nsub, 1, MXU_TILE), F32),
            pltpu.VMEM((SEQ_TILE, MXU_TILE), F32),
            pltpu.VMEM((SEQ_TILE, MXU_TILE), BF16),
            pltpu.VMEM((1, MXU_TILE), F32),
        ],
        compiler_params=pltpu.CompilerParams(
            dimension_semantics=("parallel", "parallel", "arbitrary"), vmem_limit_bytes=VMEM_LIMIT),
        name="diff_attn",
    )(q1, q2, k, vt, ga, lambdas, subln)


def _mlp_kernel(h_ref, yc_ref, ya_ref, wo_ref, gm_ref, wup_ref, wdn_ref, out_ref):
    d = h_ref.shape[-1]
    dff = wup_ref.shape[-1]
    merged = (yc_ref[0].astype(F32) + ya_ref[0].astype(F32)).astype(BF16)
    h1 = h_ref[0] + _dot(merged, wo_ref[...])
    u = _rmsnorm_rows(h1, gm_ref[...], NORM_EPS).astype(BF16)
    acc = h1
    for c in range(0, dff, d):
        t = jnp.maximum(_dot(u, wup_ref[:, c:c + d]), 0.0)
        acc = acc + _dot((t * t).astype(BF16), wdn_ref[c:c + d, :])
    out_ref[0] = acc


def _out_mlp(h, yc, ya, w_o, gm, w_up, w_down, layer):
    b, lq, d = h.shape
    dff = w_up.shape[-1]
    tm = SEQ_TILE
    tok = pl.BlockSpec((1, tm, d), lambda bi, i: (bi, i, 0))
    wspec = lambda *shape: pl.BlockSpec((None,) + shape, lambda bi, i: (layer,) + (0,) * len(shape),
                                        pipeline_mode=pl.Buffered(1))
    return pl.pallas_call(
        _mlp_kernel,
        grid=(b, lq // tm),
        in_specs=[tok, tok, tok, wspec(d, d),
                  pl.BlockSpec((None, 1, d), lambda bi, i: (layer, 0, 0)),
                  wspec(d, dff), wspec(dff, d)],
        out_specs=tok,
        out_shape=jax.ShapeDtypeStruct((b, lq, d), F32),
        compiler_params=pltpu.CompilerParams(
            dimension_semantics=("parallel", "parallel"), vmem_limit_bytes=VMEM_LIMIT),
        name="out_mlp",
    )(h, yc, ya, w_o, gm, w_up, w_down)


def _rope_tables_t(lq):
    inv = 1.0 / (ROPE_THETA ** (jnp.arange(0, HEAD_DIM, 2, dtype=F32) / HEAD_DIM))
    pos = (jnp.arange(lq, dtype=jnp.int32) - PAD0).astype(F32)
    ang = pos[:, None] * inv[None, :]
    return jnp.cos(ang).T, jnp.sin(ang).T


def kernel(x, meta_tokens, norm_mix, w_in, b_glu, conv_w, conv_b, conv_ln_g, conv_ln_b, w_pw, b_pw, q_norm, k_norm, lambdas, subln, w_o, norm_mlp, w_up, w_down):
    b, s, d = x.shape
    depth = w_in.shape[0]
    lq = FRONT + s
    assert d % MXU_TILE == 0 and lq % SEQ_TILE == 0, (s, d)

    meta = jnp.broadcast_to(meta_tokens[None].astype(x.dtype), (b, N_META, d))
    h = jnp.concatenate([jnp.zeros((b, PAD0, d), x.dtype), meta, x], axis=1)
    cos_t, sin_t = _rope_tables_t(lq)

    w_in_b = w_in.astype(BF16)
    w_pw_b = w_pw.astype(BF16)
    w_o_b = w_o.astype(BF16)
    w_up_b = w_up.astype(BF16)
    w_down_b = w_down.astype(BF16)
    conv_w8 = jnp.broadcast_to(conv_w[:, :, None, :], conv_w.shape[:2] + (SUBLANES, d))
    row = lambda p: p[:, None, :]
    col = lambda p: p[:, :, None]

    for l in range(depth):
        lam_init = LAM_INIT_BASE - LAM_INIT_AMP * math.exp(-LAM_INIT_RATE * l)
        a, q1, q2, k, vt, gc, ga = _in_proj(h, row(norm_mix), w_in_b, row(b_glu), col(q_norm), col(k_norm),
                                            cos_t, sin_t, l)
        yc = _conv_branch(a, conv_w8, row(conv_b), row(conv_ln_g), row(conv_ln_b), w_pw_b, row(b_pw), gc, l)
        ya = _attention(q1, q2, k, vt, ga, lambdas, col(subln), l, lam_init)
        h = _out_mlp(h, yc, ya, w_o_b, row(norm_mlp), w_up_b, w_down_b, l)

    return h[:, FRONT:, :]
```

```python
import functools
import math

import jax
import jax.numpy as jnp
from jax import lax
from jax.experimental import pallas as pl
from jax.experimental.pallas import tpu as pltpu

F32 = jnp.float32
BF16 = jnp.bfloat16

N_META = 16
HEAD_DIM = 64
V_DIM = 2 * HEAD_DIM
SUM_ROWS = 16
VT_ROWS = V_DIM + SUM_ROWS
CONV_K = 31
ROPE_THETA = 10000.0
NORM_EPS = 1e-6
SUBLN_EPS = 1e-5
LAM_INIT_BASE, LAM_INIT_AMP, LAM_INIT_RATE = 0.8, 0.6, 0.3

MXU_TILE = 256
SUBLANES = 8
FRONT = MXU_TILE
PAD0 = FRONT - N_META
SEQ_TILE = 3 * MXU_TILE
CONV_TILE = MXU_TILE
CONV_HALO = 32
CONV_ROWS = 64
CONV_LANES = 256
VMEM_LIMIT = 56 * 1024 * 1024
NEG_BIG = -1e30
PV_LAG = 2
M_INIT = -1e29


def _sigmoid(x):
    return 1.0 / (1.0 + jnp.exp(-x))


def _dot(a, b):
    return jnp.dot(a, b, preferred_element_type=F32)


def _rmsnorm_rows(x, g, eps):
    ms = jnp.mean(x * x, axis=-1, keepdims=True)
    return x * lax.rsqrt(ms + eps) * g


def _inproj_kernel(h_ref, gmix_ref, w_ref, bglu_ref, qg_ref, kg_ref, cos_ref, sin_ref,
                   a_ref, q1_ref, q2_ref, k_ref, vt_ref, gc_ref, ga_ref, *, qscale):
    d = h_ref.shape[-1]
    tm = h_ref.shape[-2]
    nsub = tm // MXU_TILE
    u = _rmsnorm_rows(h_ref[0], gmix_ref[...], NORM_EPS).astype(BF16)
    cos_t = cos_ref[...]
    sin_t = sin_ref[...]

    def proj(off):
        return _dot(u, w_ref[:, off:off + MXU_TILE])

    for c in range(0, d, MXU_TILE):
        lin = proj(c) + bglu_ref[:, c:c + MXU_TILE]
        gate = proj(d + c) + bglu_ref[:, d + c:d + c + MXU_TILE]
        a_ref[0, :, c:c + MXU_TILE] = (lin * _sigmoid(gate)).astype(BF16)

    half = HEAD_DIM // 2

    def norm_rope(g_t, gain_col, scale):
        r = lax.rsqrt(jnp.mean(g_t * g_t, axis=0, keepdims=True) + NORM_EPS)
        if scale != 1.0:
            r = r * scale
        gn = g_t * r * gain_col
        x1, x2 = gn[:half], gn[half:]
        return x1 * cos_t - x2 * sin_t, x2 * cos_t + x1 * sin_t

    zeros_half = jnp.zeros((HEAD_DIM, tm), BF16)
    for c in range(0, d, MXU_TILE):
        q_t = proj(2 * d + c).T
        k_t = proj(3 * d + c).T
        for hh in range(MXU_TILE // V_DIM):
            r0 = hh * V_DIM
            col = c + r0
            a1, a2 = norm_rope(q_t[r0:r0 + HEAD_DIM], qg_ref[...], qscale)
            b1, b2 = norm_rope(q_t[r0 + HEAD_DIM:r0 + V_DIM], qg_ref[...], qscale)
            q1z = jnp.concatenate([a1.astype(BF16), a2.astype(BF16), zeros_half], axis=0)
            q2z = jnp.concatenate([zeros_half, b1.astype(BF16), b2.astype(BF16)], axis=0)
            for s in range(nsub):
                q1_ref[0, s, col:col + V_DIM, :] = q1z[:, s * MXU_TILE:(s + 1) * MXU_TILE]
                q2_ref[0, s, col:col + V_DIM, :] = q2z[:, s * MXU_TILE:(s + 1) * MXU_TILE]
            a1, a2 = norm_rope(k_t[r0:r0 + HEAD_DIM], kg_ref[...], 1.0)
            b1, b2 = norm_rope(k_t[r0 + HEAD_DIM:r0 + V_DIM], kg_ref[...], 1.0)
            kh_t = jnp.concatenate([a1, a2, b1, b2], axis=0)
            k_ref[0, :, col:col + V_DIM] = kh_t.T.astype(BF16)

    ones = jnp.ones((SUM_ROWS, MXU_TILE), BF16)
    for c in range(0, d, MXU_TILE):
        v_t = proj(4 * d + c).T.astype(BF16)
        for hh in range(MXU_TILE // V_DIM):
            r0 = (c // V_DIM + hh) * VT_ROWS
            for s in range(nsub):
                vt_ref[0, s, r0:r0 + V_DIM, :] = v_t[hh * V_DIM:(hh + 1) * V_DIM, s * MXU_TILE:(s + 1) * MXU_TILE]
                vt_ref[0, s, r0 + V_DIM:r0 + VT_ROWS, :] = ones

    for c in range(0, d, MXU_TILE):
        gc_ref[0, :, c:c + MXU_TILE] = _sigmoid(proj(5 * d + c)).astype(BF16)
        ga_ref[0, :, c:c + MXU_TILE] = _sigmoid(proj(6 * d + c)).astype(BF16)


def _in_proj(h, gmix, w_in, b_glu, qg, kg, cos_t, sin_t, layer):
    b, lq, d = h.shape
    tm = SEQ_TILE
    nt = lq // tm
    nblk = lq // MXU_TILE
    n_in = w_in.shape[-1]
    qscale = (HEAD_DIM ** -0.5) * math.log2(math.e)
    tok = pl.BlockSpec((1, tm, d), lambda bi, i: (bi, i, 0))
    tsp = pl.BlockSpec((1, tm // MXU_TILE, d, MXU_TILE), lambda bi, i: (bi, i, 0, 0))
    lay = lambda *shape: pl.BlockSpec((None,) + shape, lambda bi, i: (layer,) + (0,) * len(shape))
    tok_shape = jax.ShapeDtypeStruct((b, lq, d), BF16)
    tsp_shape = jax.ShapeDtypeStruct((b, nblk, d, MXU_TILE), BF16)
    vt_rows = (d // V_DIM) * VT_ROWS
    vsp = pl.BlockSpec((1, tm // MXU_TILE, vt_rows, MXU_TILE), lambda bi, i: (bi, i, 0, 0))
    vsp_shape = jax.ShapeDtypeStruct((b, nblk, vt_rows, MXU_TILE), BF16)
    return pl.pallas_call(
        functools.partial(_inproj_kernel, qscale=qscale),
        grid=(b, nt),
        in_specs=[
            tok,
            lay(1, d),
            pl.BlockSpec((None, d, n_in), lambda bi, i: (layer, 0, 0), pipeline_mode=pl.Buffered(1)),
            lay(1, 2 * d),
            lay(HEAD_DIM, 1),
            lay(HEAD_DIM, 1),
            pl.BlockSpec((HEAD_DIM // 2, tm), lambda bi, i: (0, i)),
            pl.BlockSpec((HEAD_DIM // 2, tm), lambda bi, i: (0, i)),
        ],
        out_specs=[tok, tsp, tsp, tok, vsp, tok, tok],
        out_shape=[tok_shape, tsp_shape, tsp_shape, tok_shape, vsp_shape, tok_shape, tok_shape],
        compiler_params=pltpu.CompilerParams(
            dimension_semantics=("parallel", "parallel"), vmem_limit_bytes=VMEM_LIMIT),
        name="in_proj",
    )(h, gmix, w_in, b_glu, qg, kg, cos_t, sin_t)


def _conv_kernel(cur_ref, halo_ref, cw_ref, cb_ref, lng_ref, lnb_ref, wpw_ref, bpw_ref, gc_ref,
                 out_ref, z_ref, y_ref):
    i = pl.program_id(1)
    tt = cur_ref.shape[-2]
    d = cur_ref.shape[-1]
    hrow = lax.broadcasted_iota(jnp.int32, (CONV_HALO, 1), 0) + (i * tt - CONV_HALO)
    halo = jnp.where(hrow >= PAD0, halo_ref[0].astype(F32), 0.0)
    row = lax.broadcasted_iota(jnp.int32, (tt, 1), 0) + i * tt
    cur = jnp.where(row >= PAD0, cur_ref[0].astype(F32), 0.0)
    z_ref[0, 0:CONV_HALO, :] = halo
    z_ref[0, CONV_HALO:CONV_HALO + tt, :] = cur
    nshift = tt + CONV_HALO - SUBLANES
    for sh in range(1, SUBLANES):
        z_ref[sh, 0:nshift, :] = z_ref[0, sh:sh + nshift, :]
    base = CONV_HALO - (CONV_K - 1)
    for lc in range(0, d, CONV_LANES):
        for rc in range(0, tt, CONV_ROWS):
            acc = jnp.zeros((CONV_ROWS, CONV_LANES), F32)
            for j in range(CONV_K):
                a8, b8 = divmod(base + j, SUBLANES)
                r0 = rc + a8 * SUBLANES
                w_rows = jnp.tile(cw_ref[j, :, lc:lc + CONV_LANES], (CONV_ROWS // SUBLANES, 1))
                acc = acc + z_ref[b8, r0:r0 + CONV_ROWS, lc:lc + CONV_LANES] * w_rows
            y_ref[rc:rc + CONV_ROWS, lc:lc + CONV_LANES] = acc
    y = y_ref[...] + cb_ref[...]
    mu = jnp.mean(y, axis=-1, keepdims=True)
    yc = y - mu
    var = jnp.mean(yc * yc, axis=-1, keepdims=True)
    yn = yc * lax.rsqrt(var + NORM_EPS) * lng_ref[...] + lnb_ref[...]
    act = (yn * _sigmoid(yn)).astype(BF16)
    yo = _dot(act, wpw_ref[...]) + bpw_ref[...]
    out_ref[0] = (yo * gc_ref[0].astype(F32)).astype(BF16)


def _conv_branch(a, conv_w, conv_b, ln_g, ln_b, w_pw, b_pw, gc, layer):
    b, lq, d = a.shape
    tt = CONV_TILE
    hb = tt // CONV_HALO
    tok = pl.BlockSpec((1, tt, d), lambda bi, i: (bi, i, 0))
    lay = lambda *shape: pl.BlockSpec((None,) + shape, lambda bi, i: (layer,) + (0,) * len(shape))
    return pl.pallas_call(
        _conv_kernel,
        grid=(b, lq // tt),
        in_specs=[
            tok,
            pl.BlockSpec((1, CONV_HALO, d), lambda bi, i: (bi, jnp.maximum(i * hb - 1, 0), 0)),
            lay(CONV_K, SUBLANES, d), lay(1, d), lay(1, d), lay(1, d), lay(d, d), lay(1, d),
            tok,
        ],
        out_specs=tok,
        out_shape=jax.ShapeDtypeStruct((b, lq, d), BF16),
        scratch_shapes=[pltpu.VMEM((SUBLANES, tt + CONV_HALO, d), F32), pltpu.VMEM((tt, d), F32)],
        compiler_params=pltpu.CompilerParams(
            dimension_semantics=("parallel", "parallel"), vmem_limit_bytes=VMEM_LIMIT),
        name="conv_branch",
    )(a, a, conv_w, conv_b, ln_g, ln_b, w_pw, b_pw, gc)


FAC_ROWS = SEQ_TILE // MXU_TILE + 1


def _key_groups(nt):
    return [[t] for t in range(nt)]

def _attn_kernel(q1_ref, q2_ref, k_ref, vt_ref, ga_ref, lam_ref, sub_ref, out_ref,
                 acc_ref, m_ref, s_pend_ref, p_pend_ref, a_pend_ref, *, lam_init):
    i = pl.program_id(2)
    nsub = SEQ_TILE // MXU_TILE
    q_refs = (q1_ref, q2_ref)

    m_ref[...] = jnp.full(m_ref.shape, M_INIT, F32)
    acc_ref[...] = jnp.zeros(acc_ref.shape, F32)
    s_pend_ref[...] = jnp.full(s_pend_ref.shape, NEG_BIG, F32)
    p_pend_ref[...] = jnp.zeros(p_pend_ref.shape, BF16)
    a_pend_ref[...] = jnp.ones(a_pend_ref.shape, F32)

    def softmax_stage(c, n, s):
        lead = -s.shape[0] % MXU_TILE
        groups = _key_groups((s.shape[0] + lead) // MXU_TILE)
        parts = [s[max(g[0] * MXU_TILE - lead, 0):(g[-1] + 1) * MXU_TILE - lead] for g in groups]
        m_parts = [jnp.max(x, axis=0, keepdims=True) for x in parts]
        probs = [jnp.exp2(x - mp).astype(BF16) for x, mp in zip(parts, m_parts)]
        if lead:
            probs = [jnp.zeros((lead, MXU_TILE), BF16)] + probs
        pb = jnp.concatenate(probs, axis=0)
        m_old = m_ref[c, n]
        m_new = functools.reduce(jnp.maximum, m_parts, m_old)
        m_ref[c, n] = m_new
        fac = [jnp.exp2(mp - m_new) for mp in m_parts] + [jnp.exp2(m_old - m_new)]
        fac += [fac[-1]] * (FAC_ROWS - len(fac))
        return jnp.concatenate(fac, axis=0), pb

    def pv_stage(c, n, fac, pb, vt_tiles):
        acc = fac[FAC_ROWS - 1:FAC_ROWS] * acc_ref[c, n]
        for gi, g in enumerate(_key_groups(len(vt_tiles))):
            pv = _dot(vt_tiles[g[0]], pb[g[0] * MXU_TILE:(g[0] + 1) * MXU_TILE])
            for t in g[1:]:
                pv = pv + _dot(vt_tiles[t], pb[t * MXU_TILE:(t + 1) * MXU_TILE])
            acc = acc + fac[gi:gi + 1] * pv
        acc_ref[c, n] = acc

    def vt_block(j):
        return [vt_ref[0, j * nsub + t] for t in range(nsub)]

    block_order = [(c, n) for n in range(nsub) for c in range(2)]
    pend_order = block_order[-PV_LAG:]

    def run_block(combos, vt_prev, last):
        ncomb = len(combos)
        scores, probs = {-1: None}, {}

        def meta(t):
            return (pend_order[t] + (vt_prev,)) if t < 0 else (combos[t][0], combos[t][1], combos[t][3])

        def take_probs(t):
            if t < -1:
                return a_pend_ref[t + PV_LAG], p_pend_ref[t + PV_LAG]
            return probs.pop(t)

        def softmax_of(t):
            c, n, _ = meta(t)
            s = s_pend_ref[...] if t == -1 else scores.pop(t)
            probs[t] = softmax_stage(c, n, s)

        def pv_of(t):
            c, n, vt_tiles = meta(t)
            pv_stage(c, n, *take_probs(t), vt_tiles)

        for t in range(ncomb):
            scores[t] = combos[t][2]()
            pv_of(t - PV_LAG)
            softmax_of(t - 1)
        if last:
            for t in range(ncomb - PV_LAG, ncomb - 1):
                pv_of(t)
            softmax_of(ncomb - 1)
            pv_of(ncomb - 1)
        else:
            assert [cmb[:2] for cmb in combos[-PV_LAG:]] == pend_order
            for slot, t in enumerate(range(ncomb - PV_LAG, ncomb - 1)):
                a_pend_ref[slot], p_pend_ref[slot] = probs.pop(t)
            s_last = scores.pop(ncomb - 1)
            lead = SEQ_TILE - s_last.shape[0]
            if lead:
                s_pend_ref[0:lead, :] = jnp.full((lead, MXU_TILE), NEG_BIG, F32)
            s_pend_ref[lead:, :] = s_last

    def block_combos(j, skip_rows=0):
        kb = k_ref[0, pl.ds(pl.multiple_of(j * SEQ_TILE, SEQ_TILE), SEQ_TILE), :][skip_rows:]

        def score_fn(c, n):
            return _dot(kb, q_refs[c][0, n])

        return [(c, n, functools.partial(score_fn, c, n), vt_block(j)) for n in range(nsub) for c in range(2)]

    @pl.when(i > 0)
    def _():
        run_block(block_combos(0, PAD0), vt_block(0), last=False)

    def pair_body(p, carry):
        j = 1 + 2 * p
        run_block(block_combos(j) + block_combos(j + 1), vt_block(j - 1), last=False)
        return carry

    lax.fori_loop(0, (i - 1) // 2, pair_body, 0)

    @pl.when((i > 1) & ((i - 1) % 2 == 1))
    def _():
        run_block(block_combos(i - 1), vt_block(i - 2), last=False)

    base = pl.multiple_of(i * SEQ_TILE, SEQ_TILE)
    krow = lax.broadcasted_iota(jnp.int32, (MXU_TILE, MXU_TILE), 0)
    qcol = lax.broadcasted_iota(jnp.int32, (MXU_TILE, MXU_TILE), 1)
    causal = krow <= qcol
    pad_end = jnp.where(i == 0, PAD0, 0)
    combos = []
    for n in reversed(range(nsub)):
        nk = (n + 1) * MXU_TILE
        kb = k_ref[0, pl.ds(base, nk), :]
        first_ok = (krow >= pad_end) | (qcol < pad_end) if n == 0 else krow >= pad_end

        def score_fn(c, n=n, kb=kb, first_ok=first_ok):
            s = _dot(kb, q_refs[c][0, n])
            tiles = [s[t * MXU_TILE:(t + 1) * MXU_TILE] for t in range(n + 1)]
            tiles[n] = jnp.where(causal, tiles[n], NEG_BIG)
            tiles[0] = jnp.where(first_ok, tiles[0], NEG_BIG)
            return jnp.concatenate(tiles, axis=0)

        combos += [(c, n, functools.partial(score_fn, c), vt_block(i)[:n + 1]) for c in range(2)]

    lp = lam_ref[...]
    lam = (jnp.exp(jnp.sum(lp[0:1] * lp[1:2], axis=1, keepdims=True))
           - jnp.exp(jnp.sum(lp[2:3] * lp[3:4], axis=1, keepdims=True)) + lam_init)

    def finalize(n):
        inv_l1 = 1.0 / acc_ref[0, n, V_DIM:V_DIM + 1, :]
        inv_l2 = 1.0 / acc_ref[1, n, V_DIM:V_DIM + 1, :]
        o = acc_ref[0, n, 0:V_DIM, :] * inv_l1 - lam * (acc_ref[1, n, 0:V_DIM, :] * inv_l2)
        r = lax.rsqrt(jnp.mean(o * o, axis=0, keepdims=True) + SUBLN_EPS)
        on = o * r * sub_ref[...] * (1.0 - lam_init)
        gate = ga_ref[0, n * MXU_TILE:(n + 1) * MXU_TILE, :].astype(F32)
        out_ref[0, n * MXU_TILE:(n + 1) * MXU_TILE, :] = (on.T * gate).astype(BF16)

    run_block(combos, vt_block(jnp.maximum(i - 1, 0)), last=True)
    for n in range(nsub):
        finalize(n)


def _attention(q1, q2, k, vt, ga, lambdas, subln, layer, lam_init):
    b, lq, d = k.shape
    nh = d // V_DIM
    nq = lq // SEQ_TILE
    nblk = lq // MXU_TILE
    nsub = SEQ_TILE // MXU_TILE
    qspec = pl.BlockSpec((1, nsub, V_DIM, MXU_TILE), lambda bi, h, i: (bi, i, h, 0))
    tokh = pl.BlockSpec((1, SEQ_TILE, V_DIM), lambda bi, h, i: (bi, i, h))
    return pl.pallas_call(
        functools.partial(_attn_kernel, lam_init=lam_init),
        grid=(b, nh, nq),
        in_specs=[
            qspec, qspec,
            pl.BlockSpec((1, lq, V_DIM), lambda bi, h, i: (bi, 0, h)),
            pl.BlockSpec((1, nblk, VT_ROWS, MXU_TILE), lambda bi, h, i: (bi, 0, h, 0)),
            tokh,
            pl.BlockSpec((None, 4, HEAD_DIM), lambda bi, h, i: (layer, 0, 0)),
            pl.BlockSpec((None, V_DIM, 1), lambda bi, h, i: (layer, 0, 0)),
        ],
        out_specs=tokh,
        out_shape=jax.ShapeDtypeStruct((b, lq, d), BF16),
        scratch_shapes=[
            pltpu.VMEM((2, nsub, VT_ROWS, MXU_TILE), F32),
            pltpu.VMEM((2, nsub, 1, MXU_TILE), F32),
            pltpu.VMEM((SEQ_TILE, MXU_TILE), F32),
            pltpu.VMEM((PV_LAG - 1, SEQ_TILE, MXU_TILE), BF16),
            pltpu.VMEM((PV_LAG - 1, FAC_ROWS, MXU_TILE), F32),
        ],
        compiler_params=pltpu.CompilerParams(
            dimension_semantics=("parallel", "parallel", "arbitrary"), vmem_limit_bytes=VMEM_LIMIT),
        name="diff_attn",
    )(q1, q2, k, vt, ga, lambdas, subln)


def _mlp_kernel(h_ref, yc_ref, ya_ref, wo_ref, gm_ref, wup_ref, wdn_ref, out_ref):
    d = h_ref.shape[-1]
    dff = wup_ref.shape[-1]
    merged = (yc_ref[0].astype(F32) + ya_ref[0].astype(F32)).astype(BF16)
    h1 = h_ref[0] + _dot(merged, wo_ref[...])
    u = _rmsnorm_rows(h1, gm_ref[...], NORM_EPS).astype(BF16)
    acc = h1
    for c in range(0, dff, d):
        t = jnp.maximum(_dot(u, wup_ref[:, c:c + d]), 0.0)
        acc = acc + _dot((t * t).astype(BF16), wdn_ref[c:c + d, :])
    out_ref[0] = acc


def _out_mlp(h, yc, ya, w_o, gm, w_up, w_down, layer):
    b, lq, d = h.shape
    dff = w_up.shape[-1]
    tm = SEQ_TILE
    tok = pl.BlockSpec((1, tm, d), lambda bi, i: (bi, i, 0))
    wspec = lambda *shape: pl.BlockSpec((None,) + shape, lambda bi, i: (layer,) + (0,) * len(shape),
                                        pipeline_mode=pl.Buffered(1))
    return pl.pallas_call(
        _mlp_kernel,
        grid=(b, lq // tm),
        in_specs=[tok, tok, tok, wspec(d, d),
                  pl.BlockSpec((None, 1, d), lambda bi, i: (layer, 0, 0)),
                  wspec(d, dff), wspec(dff, d)],
        out_specs=tok,
        out_shape=jax.ShapeDtypeStruct((b, lq, d), F32),
        compiler_params=pltpu.CompilerParams(
            dimension_semantics=("parallel", "parallel"), vmem_limit_bytes=VMEM_LIMIT),
        name="out_mlp",
    )(h, yc, ya, w_o, gm, w_up, w_down)


def _rope_tables_t(lq):
    inv = 1.0 / (ROPE_THETA ** (jnp.arange(0, HEAD_DIM, 2, dtype=F32) / HEAD_DIM))
    pos = (jnp.arange(lq, dtype=jnp.int32) - PAD0).astype(F32)
    ang = pos[:, None] * inv[None, :]
    return jnp.cos(ang).T, jnp.sin(ang).T


def kernel(x, meta_tokens, norm_mix, w_in, b_glu, conv_w, conv_b, conv_ln_g, conv_ln_b, w_pw, b_pw, q_norm, k_norm, lambdas, subln, w_o, norm_mlp, w_up, w_down):
    b, s, d = x.shape
    depth = w_in.shape[0]
    lq = FRONT + s
    assert d % MXU_TILE == 0 and lq % SEQ_TILE == 0, (s, d)

    meta = jnp.broadcast_to(meta_tokens[None].astype(x.dtype), (b, N_META, d))
    h = jnp.concatenate([jnp.zeros((b, PAD0, d), x.dtype), meta, x], axis=1)
    cos_t, sin_t = _rope_tables_t(lq)

    w_in_b = w_in.astype(BF16)
    w_pw_b = w_pw.astype(BF16)
    w_o_b = w_o.astype(BF16)
    w_up_b = w_up.astype(BF16)
    w_down_b = w_down.astype(BF16)
    conv_w8 = jnp.broadcast_to(conv_w[:, :, None, :], conv_w.shape[:2] + (SUBLANES, d))
    row = lambda p: p[:, None, :]
    col = lambda p: p[:, :, None]

    for l in range(depth):
        lam_init = LAM_INIT_BASE - LAM_INIT_AMP * math.exp(-LAM_INIT_RATE * l)
        a, q1, q2, k, vt, gc, ga = _in_proj(h, row(norm_mix), w_in_b, row(b_glu), col(q_norm), col(k_norm),
                                            cos_t, sin_t, l)
        yc = _conv_branch(a, conv_w8, row(conv_b), row(conv_ln_g), row(conv_ln_b), w_pw_b, row(b_pw), gc, l)
        ya = _attention(q1, q2, k, vt, ga, lambdas, col(subln), l, lam_init)
        h = _out_mlp(h, yc, ya, w_o_b, row(norm_mlp), w_up_b, w_down_b, l)

    return h[:, FRONT:, :]
```

```python
import functools
import math

import jax
import jax.numpy as jnp
from jax import lax
from jax.experimental import pallas as pl
from jax.experimental.pallas import tpu as pltpu

F32 = jnp.float32
BF16 = jnp.bfloat16

N_META = 16
HEAD_DIM = 64
V_DIM = 2 * HEAD_DIM
SUM_ROWS = 16
VT_ROWS = V_DIM + SUM_ROWS
CONV_K = 31
ROPE_THETA = 10000.0
NORM_EPS = 1e-6
SUBLN_EPS = 1e-5
LAM_INIT_BASE, LAM_INIT_AMP, LAM_INIT_RATE = 0.8, 0.6, 0.3

MXU_TILE = 256
SUBLANES = 8
FRONT = MXU_TILE
PAD0 = FRONT - N_META
SEQ_TILE = 3 * MXU_TILE
CONV_TILE = MXU_TILE
CONV_HALO = 32
CONV_ROWS = 64
CONV_LANES = 256
VMEM_LIMIT = 56 * 1024 * 1024
NEG_BIG = -1e30
PV_LAG = 2
M_INIT = -1e29


def _sigmoid(x):
    return 1.0 / (1.0 + jnp.exp(-x))


def _dot(a, b):
    return jnp.dot(a, b, preferred_element_type=F32)


def _rmsnorm_rows(x, g, eps):
    ms = jnp.mean(x * x, axis=-1, keepdims=True)
    return x * lax.rsqrt(ms + eps) * g


def _inproj_kernel(h_ref, gmix_ref, w_ref, bglu_ref, qg_ref, kg_ref, cos_ref, sin_ref,
                   a_ref, q1_ref, q2_ref, k_ref, vt_ref, gc_ref, ga_ref, *, qscale):
    d = h_ref.shape[-1]
    tm = h_ref.shape[-2]
    nsub = tm // MXU_TILE
    u = _rmsnorm_rows(h_ref[0], gmix_ref[...], NORM_EPS).astype(BF16)
    cos_t = cos_ref[...]
    sin_t = sin_ref[...]

    def proj(off):
        return _dot(u, w_ref[:, off:off + MXU_TILE])

    for c in range(0, d, MXU_TILE):
        lin = proj(c) + bglu_ref[:, c:c + MXU_TILE]
        gate = proj(d + c) + bglu_ref[:, d + c:d + c + MXU_TILE]
        a_ref[0, :, c:c + MXU_TILE] = (lin * _sigmoid(gate)).astype(BF16)

    half = HEAD_DIM // 2

    def norm_rope(g_t, gain_col, scale):
        r = lax.rsqrt(jnp.mean(g_t * g_t, axis=0, keepdims=True) + NORM_EPS)
        if scale != 1.0:
            r = r * scale
        gn = g_t * r * gain_col
        x1, x2 = gn[:half], gn[half:]
        return x1 * cos_t - x2 * sin_t, x2 * cos_t + x1 * sin_t

    zeros_half = jnp.zeros((HEAD_DIM, tm), BF16)
    for c in range(0, d, MXU_TILE):
        q_t = proj(2 * d + c).T
        k_t = proj(3 * d + c).T
        for hh in range(MXU_TILE // V_DIM):
            r0 = hh * V_DIM
            col = c + r0
            a1, a2 = norm_rope(q_t[r0:r0 + HEAD_DIM], qg_ref[...], qscale)
            b1, b2 = norm_rope(q_t[r0 + HEAD_DIM:r0 + V_DIM], qg_ref[...], qscale)
            q1z = jnp.concatenate([a1.astype(BF16), a2.astype(BF16), zeros_half], axis=0)
            q2z = jnp.concatenate([zeros_half, b1.astype(BF16), b2.astype(BF16)], axis=0)
            for s in range(nsub):
                q1_ref[0, s, col:col + V_DIM, :] = q1z[:, s * MXU_TILE:(s + 1) * MXU_TILE]
                q2_ref[0, s, col:col + V_DIM, :] = q2z[:, s * MXU_TILE:(s + 1) * MXU_TILE]
            a1, a2 = norm_rope(k_t[r0:r0 + HEAD_DIM], kg_ref[...], 1.0)
            b1, b2 = norm_rope(k_t[r0 + HEAD_DIM:r0 + V_DIM], kg_ref[...], 1.0)
            kh_t = jnp.concatenate([a1, a2, b1, b2], axis=0)
            k_ref[0, :, col:col + V_DIM] = kh_t.T.astype(BF16)

    ones = jnp.ones((SUM_ROWS, MXU_TILE), BF16)
    for c in range(0, d, MXU_TILE):
        v_t = proj(4 * d + c).T.astype(BF16)
        for hh in range(MXU_TILE // V_DIM):
            r0 = (c // V_DIM + hh) * VT_ROWS
            for s in range(nsub):
                vt_ref[0, s, r0:r0 + V_DIM, :] = v_t[hh * V_DIM:(hh + 1) * V_DIM, s * MXU_TILE:(s + 1) * MXU_TILE]
                vt_ref[0, s, r0 + V_DIM:r0 + VT_ROWS, :] = ones

    for c in range(0, d, MXU_TILE):
        gc_ref[0, :, c:c + MXU_TILE] = _sigmoid(proj(5 * d + c)).astype(BF16)
        ga_ref[0, :, c:c + MXU_TILE] = _sigmoid(proj(6 * d + c)).astype(BF16)


def _in_proj(h, gmix, w_in, b_glu, qg, kg, cos_t, sin_t, layer):
    b, lq, d = h.shape
    tm = SEQ_TILE
    nt = lq // tm
    nblk = lq // MXU_TILE
    n_in = w_in.shape[-1]
    qscale = (HEAD_DIM ** -0.5) * math.log2(math.e)
    tok = pl.BlockSpec((1, tm, d), lambda bi, i: (bi, i, 0))
    tsp = pl.BlockSpec((1, tm // MXU_TILE, d, MXU_TILE), lambda bi, i: (bi, i, 0, 0))
    lay = lambda *shape: pl.BlockSpec((None,) + shape, lambda bi, i: (layer,) + (0,) * len(shape))
    tok_shape = jax.ShapeDtypeStruct((b, lq, d), BF16)
    tsp_shape = jax.ShapeDtypeStruct((b, nblk, d, MXU_TILE), BF16)
    vt_rows = (d // V_DIM) * VT_ROWS
    vsp = pl.BlockSpec((1, tm // MXU_TILE, vt_rows, MXU_TILE), lambda bi, i: (bi, i, 0, 0))
    vsp_shape = jax.ShapeDtypeStruct((b, nblk, vt_rows, MXU_TILE), BF16)
    return pl.pallas_call(
        functools.partial(_inproj_kernel, qscale=qscale),
        grid=(b, nt),
        in_specs=[
            tok,
            lay(1, d),
            pl.BlockSpec((None, d, n_in), lambda bi, i: (layer, 0, 0), pipeline_mode=pl.Buffered(1)),
            lay(1, 2 * d),
            lay(HEAD_DIM, 1),
            lay(HEAD_DIM, 1),
            pl.BlockSpec((HEAD_DIM // 2, tm), lambda bi, i: (0, i)),
            pl.BlockSpec((HEAD_DIM // 2, tm), lambda bi, i: (0, i)),
        ],
        out_specs=[tok, tsp, tsp, tok, vsp, tok, tok],
        out_shape=[tok_shape, tsp_shape, tsp_shape, tok_shape, vsp_shape, tok_shape, tok_shape],
        compiler_params=pltpu.CompilerParams(
            dimension_semantics=("parallel", "parallel"), vmem_limit_bytes=VMEM_LIMIT),
        name="in_proj",
    )(h, gmix, w_in, b_glu, qg, kg, cos_t, sin_t)


def _conv_kernel(cur_ref, halo_ref, cw_ref, cb_ref, lng_ref, lnb_ref, wpw_ref, bpw_ref, gc_ref,
                 out_ref, z_ref, y_ref):
    i = pl.program_id(1)
    tt = cur_ref.shape[-2]
    d = cur_ref.shape[-1]
    hrow = lax.broadcasted_iota(jnp.int32, (CONV_HALO, 1), 0) + (i * tt - CONV_HALO)
    halo = jnp.where(hrow >= PAD0, halo_ref[0].astype(F32), 0.0)
    row = lax.broadcasted_iota(jnp.int32, (tt, 1), 0) + i * tt
    cur = jnp.where(row >= PAD0, cur_ref[0].astype(F32), 0.0)
    z_ref[0, 0:CONV_HALO, :] = halo
    z_ref[0, CONV_HALO:CONV_HALO + tt, :] = cur
    nshift = tt + CONV_HALO - SUBLANES
    for sh in range(1, SUBLANES):
        z_ref[sh, 0:nshift, :] = z_ref[0, sh:sh + nshift, :]
    base = CONV_HALO - (CONV_K - 1)
    for lc in range(0, d, CONV_LANES):
        for rc in range(0, tt, CONV_ROWS):
            acc = jnp.zeros((CONV_ROWS, CONV_LANES), F32)
            for j in range(CONV_K):
                a8, b8 = divmod(base + j, SUBLANES)
                r0 = rc + a8 * SUBLANES
                w_rows = jnp.tile(cw_ref[j, :, lc:lc + CONV_LANES], (CONV_ROWS // SUBLANES, 1))
                acc = acc + z_ref[b8, r0:r0 + CONV_ROWS, lc:lc + CONV_LANES] * w_rows
            y_ref[rc:rc + CONV_ROWS, lc:lc + CONV_LANES] = acc
    y = y_ref[...] + cb_ref[...]
    mu = jnp.mean(y, axis=-1, keepdims=True)
    yc = y - mu
    var = jnp.mean(yc * yc, axis=-1, keepdims=True)
    yn = yc * lax.rsqrt(var + NORM_EPS) * lng_ref[...] + lnb_ref[...]
    act = (yn * _sigmoid(yn)).astype(BF16)
    yo = _dot(act, wpw_ref[...]) + bpw_ref[...]
    out_ref[0] = (yo * gc_ref[0].astype(F32)).astype(BF16)


def _conv_branch(a, conv_w, conv_b, ln_g, ln_b, w_pw, b_pw, gc, layer):
    b, lq, d = a.shape
    tt = CONV_TILE
    hb = tt // CONV_HALO
    tok = pl.BlockSpec((1, tt, d), lambda bi, i: (bi, i, 0))
    lay = lambda *shape: pl.BlockSpec((None,) + shape, lambda bi, i: (layer,) + (0,) * len(shape))
    return pl.pallas_call(
        _conv_kernel,
        grid=(b, lq // tt),
        in_specs=[
            tok,
            pl.BlockSpec((1, CONV_HALO, d), lambda bi, i: (bi, jnp.maximum(i * hb - 1, 0), 0)),
            lay(CONV_K, SUBLANES, d), lay(1, d), lay(1, d), lay(1, d), lay(d, d), lay(1, d),
            tok,
        ],
        out_specs=tok,
        out_shape=jax.ShapeDtypeStruct((b, lq, d), BF16),
        scratch_shapes=[pltpu.VMEM((SUBLANES, tt + CONV_HALO, d), F32), pltpu.VMEM((tt, d), F32)],
        compiler_params=pltpu.CompilerParams(
            dimension_semantics=("parallel", "parallel"), vmem_limit_bytes=VMEM_LIMIT),
        name="conv_branch",
    )(a, a, conv_w, conv_b, ln_g, ln_b, w_pw, b_pw, gc)


FAC_ROWS = SEQ_TILE // MXU_TILE + 1


def _key_groups(nt):
    return [[t] for t in range(nt)]

def _attn_kernel(q1_ref, q2_ref, k_ref, vt_ref, ga_ref, lam_ref, sub_ref, out_ref,
                 acc_ref, m_ref, s_pend_ref, p_pend_ref, a_pend_ref, *, lam_init):
    i = pl.program_id(2)
    nsub = SEQ_TILE // MXU_TILE
    q_refs = (q1_ref, q2_ref)

    m_ref[...] = jnp.full(m_ref.shape, M_INIT, F32)
    acc_ref[...] = jnp.zeros(acc_ref.shape, F32)

    def softmax_stage(c, n, s):
        lead = -s.shape[0] % MXU_TILE
        groups = _key_groups((s.shape[0] + lead) // MXU_TILE)
        parts = [s[max(g[0] * MXU_TILE - lead, 0):(g[-1] + 1) * MXU_TILE - lead] for g in groups]
        m_parts = [jnp.max(x, axis=0, keepdims=True) for x in parts]
        probs = [jnp.exp2(x - mp).astype(BF16) for x, mp in zip(parts, m_parts)]
        if lead:
            probs = [jnp.zeros((lead, MXU_TILE), BF16)] + probs
        pb = jnp.concatenate(probs, axis=0)
        m_old = m_ref[c, n]
        m_new = functools.reduce(jnp.maximum, m_parts, m_old)
        m_ref[c, n] = m_new
        fac = [jnp.exp2(mp - m_new) for mp in m_parts] + [jnp.exp2(m_old - m_new)]
        fac += [fac[-1]] * (FAC_ROWS - len(fac))
        return jnp.concatenate(fac, axis=0), pb

    def pv_stage(c, n, fac, pb, vt_tiles):
        acc = fac[FAC_ROWS - 1:FAC_ROWS] * acc_ref[c, n]
        for gi, g in enumerate(_key_groups(len(vt_tiles))):
            pv = _dot(vt_tiles[g[0]], pb[g[0] * MXU_TILE:(g[0] + 1) * MXU_TILE])
            for t in g[1:]:
                pv = pv + _dot(vt_tiles[t], pb[t * MXU_TILE:(t + 1) * MXU_TILE])
            acc = acc + fac[gi:gi + 1] * pv
        acc_ref[c, n] = acc

    def vt_block(j):
        return [vt_ref[0, j * nsub + t] for t in range(nsub)]

    block_order = [(c, n) for n in range(nsub) for c in range(2)]
    pend_order = block_order[-PV_LAG:]

    def run_block(combos, vt_prev, last, first=False):
        ncomb = len(combos)
        scores, probs = {}, {}

        def meta(t):
            return (pend_order[t] + (vt_prev,)) if t < 0 else (combos[t][0], combos[t][1], combos[t][3])

        def take_probs(t):
            if t < -1:
                return a_pend_ref[t + PV_LAG], p_pend_ref[t + PV_LAG]
            return probs.pop(t)

        def softmax_of(t):
            if first and t < 0:
                return
            c, n, _ = meta(t)
            s = s_pend_ref[...] if t == -1 else scores.pop(t)
            probs[t] = softmax_stage(c, n, s)

        def pv_of(t):
            if first and t < 0:
                return
            c, n, vt_tiles = meta(t)
            pv_stage(c, n, *take_probs(t), vt_tiles)

        for t in range(ncomb):
            scores[t] = combos[t][2]()
            pv_of(t - PV_LAG)
            softmax_of(t - 1)
        if last:
            for t in range(ncomb - PV_LAG, ncomb - 1):
                pv_of(t)
            softmax_of(ncomb - 1)
            pv_of(ncomb - 1)
        else:
            assert [cmb[:2] for cmb in combos[-PV_LAG:]] == pend_order
            for slot, t in enumerate(range(ncomb - PV_LAG, ncomb - 1)):
                a_pend_ref[slot], p_pend_ref[slot] = probs.pop(t)
            s_last = scores.pop(ncomb - 1)
            lead = SEQ_TILE - s_last.shape[0]
            if lead:
                s_pend_ref[0:lead, :] = jnp.full((lead, MXU_TILE), NEG_BIG, F32)
            s_pend_ref[lead:, :] = s_last

    def block_combos(j, skip_rows=0):
        kb = k_ref[0, pl.ds(pl.multiple_of(j * SEQ_TILE, SEQ_TILE), SEQ_TILE), :][skip_rows:]

        def score_fn(c, n):
            return _dot(kb, q_refs[c][0, n])

        return [(c, n, functools.partial(score_fn, c, n), vt_block(j)) for n in range(nsub) for c in range(2)]

    @pl.when(i > 0)
    def _():
        run_block(block_combos(0, PAD0), None, last=False, first=True)

    def pair_body(p, carry):
        j = 1 + 2 * p
        run_block(block_combos(j) + block_combos(j + 1), vt_block(j - 1), last=False)
        return carry

    lax.fori_loop(0, (i - 1) // 2, pair_body, 0)

    @pl.when((i > 1) & ((i - 1) % 2 == 1))
    def _():
        run_block(block_combos(i - 1), vt_block(i - 2), last=False)

    base = pl.multiple_of(i * SEQ_TILE, SEQ_TILE)
    krow = lax.broadcasted_iota(jnp.int32, (MXU_TILE, MXU_TILE), 0)
    qcol = lax.broadcasted_iota(jnp.int32, (MXU_TILE, MXU_TILE), 1)
    causal = krow <= qcol

    def diag_combos(has_pad):
        combos = []
        for n in reversed(range(nsub)):
            nk = (n + 1) * MXU_TILE
            kb = k_ref[0, pl.ds(base, nk), :]
            first_ok = ((krow >= PAD0) | (qcol < PAD0) if n == 0 else krow >= PAD0) if has_pad else None

            def score_fn(c, n=n, kb=kb, first_ok=first_ok):
                s = _dot(kb, q_refs[c][0, n])
                tiles = [s[t * MXU_TILE:(t + 1) * MXU_TILE] for t in range(n + 1)]
                tiles[n] = jnp.where(causal, tiles[n], NEG_BIG)
                if first_ok is not None:
                    tiles[0] = jnp.where(first_ok, tiles[0], NEG_BIG)
                return jnp.concatenate(tiles, axis=0)

            combos += [(c, n, functools.partial(score_fn, c), vt_block(i)[:n + 1]) for c in range(2)]
        return combos

    lp = lam_ref[...]
    lam = (jnp.exp(jnp.sum(lp[0:1] * lp[1:2], axis=1, keepdims=True))
           - jnp.exp(jnp.sum(lp[2:3] * lp[3:4], axis=1, keepdims=True)) + lam_init)

    def finalize(n):
        inv_l1 = 1.0 / acc_ref[0, n, V_DIM:V_DIM + 1, :]
        inv_l2 = 1.0 / acc_ref[1, n, V_DIM:V_DIM + 1, :]
        o = acc_ref[0, n, 0:V_DIM, :] * inv_l1 - lam * (acc_ref[1, n, 0:V_DIM, :] * inv_l2)
        r = lax.rsqrt(jnp.mean(o * o, axis=0, keepdims=True) + SUBLN_EPS)
        on = o * r * sub_ref[...] * (1.0 - lam_init)
        gate = ga_ref[0, n * MXU_TILE:(n + 1) * MXU_TILE, :].astype(F32)
        out_ref[0, n * MXU_TILE:(n + 1) * MXU_TILE, :] = (on.T * gate).astype(BF16)

    @pl.when(i == 0)
    def _():
        run_block(diag_combos(True), None, last=True, first=True)

    @pl.when(i > 0)
    def _():
        run_block(diag_combos(False), vt_block(i - 1), last=True)

    for n in range(nsub):
        finalize(n)


def _attention(q1, q2, k, vt, ga, lambdas, subln, layer, lam_init):
    b, lq, d = k.shape
    nh = d // V_DIM
    nq = lq // SEQ_TILE
    nblk = lq // MXU_TILE
    nsub = SEQ_TILE // MXU_TILE
    qspec = pl.BlockSpec((1, nsub, V_DIM, MXU_TILE), lambda bi, h, i: (bi, i, h, 0))
    tokh = pl.BlockSpec((1, SEQ_TILE, V_DIM), lambda bi, h, i: (bi, i, h))
    return pl.pallas_call(
        functools.partial(_attn_kernel, lam_init=lam_init),
        grid=(b, nh, nq),
        in_specs=[
            qspec, qspec,
            pl.BlockSpec((1, lq, V_DIM), lambda bi, h, i: (bi, 0, h)),
            pl.BlockSpec((1, nblk, VT_ROWS, MXU_TILE), lambda bi, h, i: (bi, 0, h, 0)),
            tokh,
            pl.BlockSpec((None, 4, HEAD_DIM), lambda bi, h, i: (layer, 0, 0)),
            pl.BlockSpec((None, V_DIM, 1), lambda bi, h, i: (layer, 0, 0)),
        ],
        out_specs=tokh,
        out_shape=jax.ShapeDtypeStruct((b, lq, d), BF16),
        scratch_shapes=[
            pltpu.VMEM((2, nsub, VT_ROWS, MXU_TILE), F32),
            pltpu.VMEM((2, nsub, 1, MXU_TILE), F32),
            pltpu.VMEM((SEQ_TILE, MXU_TILE), F32),
            pltpu.VMEM((PV_LAG - 1, SEQ_TILE, MXU_TILE), BF16),
            pltpu.VMEM((PV_LAG - 1, FAC_ROWS, MXU_TILE), F32),
        ],
        compiler_params=pltpu.CompilerParams(
            dimension_semantics=("parallel", "parallel", "arbitrary"), vmem_limit_bytes=VMEM_LIMIT),
        name="diff_attn",
    )(q1, q2, k, vt, ga, lambdas, subln)


def _mlp_kernel(h_ref, yc_ref, ya_ref, wo_ref, gm_ref, wup_ref, wdn_ref, out_ref):
    d = h_ref.shape[-1]
    dff = wup_ref.shape[-1]
    merged = (yc_ref[0].astype(F32) + ya_ref[0].astype(F32)).astype(BF16)
    h1 = h_ref[0] + _dot(merged, wo_ref[...])
    u = _rmsnorm_rows(h1, gm_ref[...], NORM_EPS).astype(BF16)
    acc = h1
    for c in range(0, dff, d):
        t = jnp.maximum(_dot(u, wup_ref[:, c:c + d]), 0.0)
        acc = acc + _dot((t * t).astype(BF16), wdn_ref[c:c + d, :])
    out_ref[0] = acc


def _out_mlp(h, yc, ya, w_o, gm, w_up, w_down, layer):
    b, lq, d = h.shape
    dff = w_up.shape[-1]
    tm = SEQ_TILE
    tok = pl.BlockSpec((1, tm, d), lambda bi, i: (bi, i, 0))
    wspec = lambda *shape: pl.BlockSpec((None,) + shape, lambda bi, i: (layer,) + (0,) * len(shape),
                                        pipeline_mode=pl.Buffered(1))
    return pl.pallas_call(
        _mlp_kernel,
        grid=(b, lq // tm),
        in_specs=[tok, tok, tok, wspec(d, d),
                  pl.BlockSpec((None, 1, d), lambda bi, i: (layer, 0, 0)),
                  wspec(d, dff), wspec(dff, d)],
        out_specs=tok,
        out_shape=jax.ShapeDtypeStruct((b, lq, d), F32),
        compiler_params=pltpu.CompilerParams(
            dimension_semantics=("parallel", "parallel"), vmem_limit_bytes=VMEM_LIMIT),
        name="out_mlp",
    )(h, yc, ya, w_o, gm, w_up, w_down)


def _rope_tables_t(lq):
    inv = 1.0 / (ROPE_THETA ** (jnp.arange(0, HEAD_DIM, 2, dtype=F32) / HEAD_DIM))
    pos = (jnp.arange(lq, dtype=jnp.int32) - PAD0).astype(F32)
    ang = pos[:, None] * inv[None, :]
    return jnp.cos(ang).T, jnp.sin(ang).T


def kernel(x, meta_tokens, norm_mix, w_in, b_glu, conv_w, conv_b, conv_ln_g, conv_ln_b, w_pw, b_pw, q_norm, k_norm, lambdas, subln, w_o, norm_mlp, w_up, w_down):
    b, s, d = x.shape
    depth = w_in.shape[0]
    lq = FRONT + s
    assert d % MXU_TILE == 0 and lq % SEQ_TILE == 0, (s, d)

    meta = jnp.broadcast_to(meta_tokens[None].astype(x.dtype), (b, N_META, d))
    h = jnp.concatenate([jnp.zeros((b, PAD0, d), x.dtype), meta, x], axis=1)
    cos_t, sin_t = _rope_tables_t(lq)

    w_in_b = w_in.astype(BF16)
    w_pw_b = w_pw.astype(BF16)
    w_o_b = w_o.astype(BF16)
    w_up_b = w_up.astype(BF16)
    w_down_b = w_down.astype(BF16)
    conv_w8 = jnp.broadcast_to(conv_w[:, :, None, :], conv_w.shape[:2] + (SUBLANES, d))
    row = lambda p: p[:, None, :]
    col = lambda p: p[:, :, None]

    for l in range(depth):
        lam_init = LAM_INIT_BASE - LAM_INIT_AMP * math.exp(-LAM_INIT_RATE * l)
        a, q1, q2, k, vt, gc, ga = _in_proj(h, row(norm_mix), w_in_b, row(b_glu), col(q_norm), col(k_norm),
                                            cos_t, sin_t, l)
        yc = _conv_branch(a, conv_w8, row(conv_b), row(conv_ln_g), row(conv_ln_b), w_pw_b, row(b_pw), gc, l)
        ya = _attention(q1, q2, k, vt, ga, lambdas, col(subln), l, lam_init)
        h = _out_mlp(h, yc, ya, w_o_b, row(norm_mlp), w_up_b, w_down_b, l)

    return h[:, FRONT:, :]
```
